```python
import math
import jax, jax.numpy as jnp
from jax import lax
import numpy as np

D_MODEL = 2048
BATCH = 4
SEQ = 2048
DEPTH = 2
DEC_BATCH = 128
DEC_SEQ = 8
PAST_LEN = 16384
PAGE_SIZE = 128

N_MIXERS = 2
N_MLSTM_LAYERS = (DEPTH + 1) // N_MIXERS
N_RWKV_LAYERS = DEPTH // N_MIXERS
PLE_DIM = 256
D_FF = -(-8 * D_MODEL // (3 * 256)) * 256

MLSTM_HEADS = 8
MLSTM_DQK = D_MODEL // 2 // MLSTM_HEADS
MLSTM_DV = D_MODEL // MLSTM_HEADS
MLSTM_CHUNK = 64
GATE_SOFTCAP = 15.0

RWKV_HEAD_DIM = 64
RWKV_HEADS = D_MODEL // RWKV_HEAD_DIM
DECAY_LORA = 96
AAA_LORA = 96
GATE_LORA = 256

NORM_EPS = 1e-6
RWKV_GN_EPS = 64e-5
L2_EPS = 1e-12

kernel_name = 'mlstm_rwkv7_hybrid_step'


def rms_norm(x, g):
    xf = x.astype(jnp.float32)
    y = xf * lax.rsqrt(jnp.mean(xf * xf, axis=-1, keepdims=True) + NORM_EPS)
    return (y * g.astype(jnp.float32)).astype(x.dtype)


def softcap(x, cap):
    return cap * jnp.tanh(x / cap)


def swiglu(x, w_gate, w_up, w_down):
    return (jax.nn.silu(x @ w_gate) * (x @ w_up)) @ w_down


def mlstm_chunk_step(carry, inp):
    C, n, m = carry
    q, k, v, li, lf = inp
    L = q.shape[2]
    b = jnp.cumsum(lf, axis=-1)
    causal = jnp.tril(jnp.ones((L, L), dtype=bool))
    dlog = jnp.where(causal, b[..., :, None] - b[..., None, :] + li[..., None, :], -jnp.inf)
    a = b + m[..., None]
    m_t = jnp.maximum(a, jnp.max(dlog, axis=-1))
    w_intra = jnp.exp(dlog - m_t[..., None])
    w_inter = jnp.exp(a - m_t)
    s = jnp.einsum('bhtd,bhsd->bhts', q, k) * w_intra
    num = w_inter[..., None] * jnp.einsum('bhtd,bhde->bhte', q, C) + jnp.einsum('bhts,bhse->bhte', s, v)
    den = w_inter * jnp.einsum('bhtd,bhd->bht', q, n) + jnp.sum(s, axis=-1)
    h = num / jnp.maximum(jnp.abs(den), jnp.exp(-m_t))[..., None]
    m_new = m_t[..., -1]
    decay_state = jnp.exp(b[..., -1] + m - m_new)
    w_k = jnp.exp(b[..., -1:] - b + li - m_new[..., None])
    C_new = decay_state[..., None, None] * C + jnp.einsum('bhs,bhsd,bhse->bhde', w_k, k, v)
    n_new = decay_state[..., None] * n + jnp.einsum('bhs,bhsd->bhd', w_k, k)
    return (C_new, n_new, m_new), h


def mlstm_mixer(xn, w_q, w_k, w_v, w_ig, b_ig, w_fg, b_fg, w_og, norm_w, w_out, C0, n0, m0):
    B, T, _ = xn.shape
    H, DK, DV = MLSTM_HEADS, MLSTM_DQK, MLSTM_DV
    f32 = jnp.float32
    q = (xn @ w_q).astype(f32).reshape(B, T, H, DK).transpose(0, 2, 1, 3)
    k = (xn @ w_k).astype(f32).reshape(B, T, H, DK).transpose(0, 2, 1, 3) * (DK ** -0.5)
    v = (xn @ w_v).astype(f32).reshape(B, T, H, DV).transpose(0, 2, 1, 3)
    li = softcap((xn @ w_ig + b_ig).astype(f32), GATE_SOFTCAP).transpose(0, 2, 1)
    lf = jax.nn.log_sigmoid(softcap((xn @ w_fg + b_fg).astype(f32), GATE_SOFTCAP)).transpose(0, 2, 1)
    L = math.gcd(T, MLSTM_CHUNK)
    NC = T // L

    def blocks(t):
        return jnp.moveaxis(t.reshape(B, H, NC, L, *t.shape[3:]), 2, 0)

    carry0 = (C0.astype(f32), n0.astype(f32), m0.astype(f32))
    (C, n, m), h = lax.scan(mlstm_chunk_step, carry0, (blocks(q), blocks(k), blocks(v), blocks(li), blocks(lf)))
    h = jnp.moveaxis(h, 0, 2).reshape(B, H, T, DV).transpose(0, 2, 1, 3)
    h = h * lax.rsqrt(jnp.mean(h * h, axis=-1, keepdims=True) + NORM_EPS)
    h = h.reshape(B, T, H * DV) * norm_w.astype(f32)
    o = jax.nn.sigmoid((xn @ w_og).astype(f32))
    out = (o * h).astype(xn.dtype) @ w_out
    return out, C, n, m


def rwkv7_step(S, inp):
    r, w, k, v, kk, a = inp
    sa = jnp.einsum('bhvk,bhk->bhv', S, -kk)
    S = S * w[:, :, None, :] + sa[..., None] * (kk * a)[:, :, None, :] + v[..., None] * k[:, :, None, :]
    return S, jnp.einsum('bhvk,bhk->bhv', S, r)


def rwkv7_mixer(xn, mu, w_r, w_k, w_v, w_o, w0, w1, w2, a0, a1, a2, g1, g2, k_k, k_a, r_k, ln_w, ln_b, S0, shift0):
    B, T, D = xn.shape
    H, N = RWKV_HEADS, RWKV_HEAD_DIM
    f32 = jnp.float32
    x_prev = jnp.concatenate([shift0[:, None, :].astype(xn.dtype), xn[:, :-1]], axis=1)
    xx = x_prev - xn
    xr, xw, xk, xv, xa, xg = [xn + xx * mu[c] for c in range(6)]
    r = (xr @ w_r).astype(f32)
    k = (xk @ w_k).astype(f32)
    v = (xv @ w_v).astype(f32)
    w_log = -jax.nn.softplus(-(w0 + jnp.tanh(xw @ w1) @ w2).astype(f32)) - 0.5
    decay = jnp.exp(-jnp.exp(w_log))
    a = jax.nn.sigmoid((a0 + (xa @ a1) @ a2).astype(f32))
    g = (jax.nn.sigmoid(xg @ g1) @ g2).astype(f32)
    kk = (k * k_k).reshape(B, T, H, N)
    kk = kk / jnp.maximum(jnp.linalg.norm(kk, axis=-1, keepdims=True), L2_EPS)
    k = k * (1.0 + (a - 1.0) * k_a)

    def heads(t):
        return jnp.moveaxis(t.reshape(B, T, H, N), 1, 0)

    S, y = lax.scan(rwkv7_step, S0.astype(f32),
                    (heads(r), heads(decay), heads(k), heads(v), jnp.moveaxis(kk, 1, 0), heads(a)))
    y = jnp.moveaxis(y, 0, 1)
    mean = jnp.mean(y, axis=-1, keepdims=True)
    var = jnp.mean(jnp.square(y - mean), axis=-1, keepdims=True)
    y = ((y - mean) * lax.rsqrt(var + RWKV_GN_EPS)).reshape(B, T, D) * ln_w + ln_b
    bonus = jnp.sum(r.reshape(B, T, H, N) * k.reshape(B, T, H, N) * r_k, axis=-1, keepdims=True) * v.reshape(B, T, H, N)
    y = y + bonus.reshape(B, T, D)
    out = (y * g).astype(xn.dtype) @ w_o
    return out, S, xn[:, -1]


def trunk(x, p, mC, mn, mm, rS, rshift, w):
    h = x
    new_C, new_n, new_m, new_S, new_shift = [], [], [], [], []
    for i in range(DEPTH):
        j = i // N_MIXERS
        hn = rms_norm(h, w['norm_mix'][i])
        if i % N_MIXERS == 0:
            out, C, n, m = mlstm_mixer(hn, w['mlstm_w_q'][j], w['mlstm_w_k'][j], w['mlstm_w_v'][j],
                                       w['mlstm_w_igate'][j], w['mlstm_b_igate'][j], w['mlstm_w_fgate'][j],
                                       w['mlstm_b_fgate'][j], w['mlstm_w_ogate'][j], w['mlstm_norm_w'][j],
                                       w['mlstm_w_out'][j], mC[j], mn[j], mm[j])
            new_C.append(C)
            new_n.append(n)
            new_m.append(m)
        else:
            out, S, sh = rwkv7_mixer(hn, w['rwkv_mu'][j], w['rwkv_w_r'][j], w['rwkv_w_k'][j], w['rwkv_w_v'][j],
                                     w['rwkv_w_o'][j], w['rwkv_w0'][j], w['rwkv_w1'][j], w['rwkv_w2'][j],
                                     w['rwkv_a0'][j], w['rwkv_a1'][j], w['rwkv_a2'][j], w['rwkv_g1'][j],
                                     w['rwkv_g2'][j], w['rwkv_k_k'][j], w['rwkv_k_a'][j], w['rwkv_r_k'][j],
                                     w['rwkv_ln_w'][j], w['rwkv_ln_b'][j], rS[j], rshift[j])
            new_S.append(S)
            new_shift.append(sh)
        h = h + out
        h = h + swiglu(rms_norm(h, w['norm_ffn'][i]), w['ffn_w_gate'][i], w['ffn_w_up'][i], w['ffn_w_down'][i])
        gate = jax.nn.sigmoid(rms_norm(h, w['norm_ple'][i]) @ w['ple_w_gate'][i])
        h = h + gate * (p[i].astype(h.dtype) @ w['ple_w_proj'][i])
    y = rms_norm(h, w['norm_final'])
    return y, jnp.stack(new_C), jnp.stack(new_n), jnp.stack(new_m), jnp.stack(new_S), jnp.stack(new_shift)


def setup_inputs(seed: int = 0) -> dict:
    key = jax.random.key(seed)
    ks = list(jax.random.split(key, 64))
    f32 = jnp.float32

    def nrm(shape, scale):
        return jax.random.normal(ks.pop(), shape, f32) * scale

    def gain(shape):
        return 1.0 + nrm(shape, 0.02)

    D = D_MODEL
    sd = D ** -0.5
    NM, NR = N_MLSTM_LAYERS, N_RWKV_LAYERS
    H, DK, DV = MLSTM_HEADS, MLSTM_DQK, MLSTM_DV
    RH, RN = RWKV_HEADS, RWKV_HEAD_DIM
    return {
        'x_prompt': nrm((BATCH, SEQ, D), 1.0),
        'x_sample': nrm((DEC_BATCH, DEC_SEQ, D), 1.0),
        'state_mlstm_C': nrm((NM, DEC_BATCH, H, DK, DV), 0.1),
        'state_mlstm_n': nrm((NM, DEC_BATCH, H, DK), 0.1),
        'state_mlstm_m': nrm((NM, DEC_BATCH, H), 1.0),
        'state_rwkv_S': nrm((NR, DEC_BATCH, RH, RN, RN), 0.1),
        'state_rwkv_shift': nrm((NR, DEC_BATCH, D), 1.0),
        'p_prompt': nrm((DEPTH, BATCH, SEQ, PLE_DIM), 1.0),
        'p_sample': nrm((DEPTH, DEC_BATCH, DEC_SEQ, PLE_DIM), 1.0),
        'norm_mix': gain((DEPTH, D)),
        'norm_ffn': gain((DEPTH, D)),
        'norm_ple': gain((DEPTH, D)),
        'norm_final': gain((D,)),
        'ffn_w_gate': nrm((DEPTH, D, D_FF), sd),
        'ffn_w_up': nrm((DEPTH, D, D_FF), sd),
        'ffn_w_down': nrm((DEPTH, D_FF, D), D_FF ** -0.5),
        'ple_w_proj': nrm((DEPTH, PLE_DIM, D), PLE_DIM ** -0.5),
        'ple_w_gate': nrm((DEPTH, D, D), sd),
        'mlstm_w_q': nrm((NM, D, H * DK), sd),
        'mlstm_w_k': nrm((NM, D, H * DK), sd),
        'mlstm_w_v': nrm((NM, D, H * DV), sd),
        'mlstm_w_igate': nrm((NM, D, H), sd),
        'mlstm_b_igate': -1.0 + nrm((NM, H), 0.1),
        'mlstm_w_fgate': nrm((NM, D, H), sd),
        'mlstm_b_fgate': 3.0 + nrm((NM, H), 0.1),
        'mlstm_w_ogate': nrm((NM, D, D), sd),
        'mlstm_norm_w': gain((NM, D)),
        'mlstm_w_out': nrm((NM, D, D), sd),
        'rwkv_mu': jax.random.uniform(ks.pop(), (NR, 6, D), f32),
        'rwkv_w_r': nrm((NR, D, D), sd),
        'rwkv_w_k': nrm((NR, D, D), sd),
        'rwkv_w_v': nrm((NR, D, D), sd),
        'rwkv_w_o': nrm((NR, D, D), sd),
        'rwkv_w0': -6.0 + 5.0 * jax.random.uniform(ks.pop(), (NR, D), f32),
        'rwkv_w1': nrm((NR, D, DECAY_LORA), sd),
        'rwkv_w2': nrm((NR, DECAY_LORA, D), 0.5 * DECAY_LORA ** -0.5),
        'rwkv_a0': nrm((NR, D), 0.1),
        'rwkv_a1': nrm((NR, D, AAA_LORA), sd),
        'rwkv_a2': nrm((NR, AAA_LORA, D), 0.5 * AAA_LORA ** -0.5),
        'rwkv_g1': nrm((NR, D, GATE_LORA), sd),
        'rwkv_g2': nrm((NR, GATE_LORA, D), GATE_LORA ** -0.5),
        'rwkv_k_k': 0.85 + nrm((NR, D), 0.05),
        'rwkv_k_a': 1.0 + nrm((NR, D), 0.05),
        'rwkv_r_k': nrm((NR, RH, RN), 0.1),
        'rwkv_ln_w': gain((NR, D)),
        'rwkv_ln_b': nrm((NR, D), 0.02),
    }


def reference(x_prompt, x_sample, state_mlstm_C, state_mlstm_n, state_mlstm_m, state_rwkv_S, state_rwkv_shift,
              p_prompt, p_sample, norm_mix, norm_ffn, norm_ple, norm_final, ffn_w_gate, ffn_w_up, ffn_w_down,
              ple_w_proj, ple_w_gate, mlstm_w_q, mlstm_w_k, mlstm_w_v, mlstm_w_igate, mlstm_b_igate,
              mlstm_w_fgate, mlstm_b_fgate, mlstm_w_ogate, mlstm_norm_w, mlstm_w_out, rwkv_mu, rwkv_w_r,
              rwkv_w_k, rwkv_w_v, rwkv_w_o, rwkv_w0, rwkv_w1, rwkv_w2, rwkv_a0, rwkv_a1, rwkv_a2, rwkv_g1,
              rwkv_g2, rwkv_k_k, rwkv_k_a, rwkv_r_k, rwkv_ln_w, rwkv_ln_b):
    w = {
        'norm_mix': norm_mix, 'norm_ffn': norm_ffn, 'norm_ple': norm_ple, 'norm_final': norm_final,
        'ffn_w_gate': ffn_w_gate, 'ffn_w_up': ffn_w_up, 'ffn_w_down': ffn_w_down,
        'ple_w_proj': ple_w_proj, 'ple_w_gate': ple_w_gate,
        'mlstm_w_q': mlstm_w_q, 'mlstm_w_k': mlstm_w_k, 'mlstm_w_v': mlstm_w_v,
        'mlstm_w_igate': mlstm_w_igate, 'mlstm_b_igate': mlstm_b_igate,
        'mlstm_w_fgate': mlstm_w_fgate, 'mlstm_b_fgate': mlstm_b_fgate,
        'mlstm_w_ogate': mlstm_w_ogate, 'mlstm_norm_w': mlstm_norm_w, 'mlstm_w_out': mlstm_w_out,
        'rwkv_mu': rwkv_mu, 'rwkv_w_r': rwkv_w_r, 'rwkv_w_k': rwkv_w_k, 'rwkv_w_v': rwkv_w_v,
        'rwkv_w_o': rwkv_w_o, 'rwkv_w0': rwkv_w0, 'rwkv_w1': rwkv_w1, 'rwkv_w2': rwkv_w2,
        'rwkv_a0': rwkv_a0, 'rwkv_a1': rwkv_a1, 'rwkv_a2': rwkv_a2, 'rwkv_g1': rwkv_g1, 'rwkv_g2': rwkv_g2,
        'rwkv_k_k': rwkv_k_k, 'rwkv_k_a': rwkv_k_a, 'rwkv_r_k': rwkv_r_k,
        'rwkv_ln_w': rwkv_ln_w, 'rwkv_ln_b': rwkv_ln_b,
    }
    f32 = jnp.float32
    B = x_prompt.shape[0]
    zC = jnp.zeros((N_MLSTM_LAYERS, B, MLSTM_HEADS, MLSTM_DQK, MLSTM_DV), f32)
    zn = jnp.zeros((N_MLSTM_LAYERS, B, MLSTM_HEADS, MLSTM_DQK), f32)
    zm = jnp.zeros((N_MLSTM_LAYERS, B, MLSTM_HEADS), f32)
    zS = jnp.zeros((N_RWKV_LAYERS, B, RWKV_HEADS, RWKV_HEAD_DIM, RWKV_HEAD_DIM), f32)
    zsh = jnp.zeros((N_RWKV_LAYERS, B, D_MODEL), x_prompt.dtype)
    y_prompt, C_p, n_p, m_p, S_p, shift_p = trunk(x_prompt, p_prompt, zC, zn, zm, zS, zsh, w)
    y_sample, C_s, n_s, m_s, S_s, shift_s = trunk(x_sample, p_sample, state_mlstm_C, state_mlstm_n,
                                                  state_mlstm_m, state_rwkv_S, state_rwkv_shift, w)
    return (y_prompt, y_sample, C_p, n_p, m_p, S_p, shift_p, C_s, n_s, m_s, S_s, shift_s)
```

```python
import functools
import math

import jax
import jax.numpy as jnp
from jax import lax
from jax.experimental import pallas as pl
from jax.experimental.pallas import tpu as pltpu

F32 = jnp.float32
BF16 = jnp.bfloat16

D_MODEL = 2048
MLSTM_HEADS = 8
MLSTM_DQK = 128
MLSTM_DV = 256
MLSTM_CHUNK = 64
GATE_SOFTCAP = 15.0
RWKV_HEAD_DIM = 64
RWKV_HEADS = 32
RWKV_PAIRS = RWKV_HEADS // 2
RWKV_CHUNK = 64
NORM_EPS = 1e-6
RWKV_GN_EPS = 64e-5
L2_EPS = 1e-12
LANES = 128
VMEM_LIMIT_BYTES = 56 * 1024 * 1024


def _params(*sem):
    return pltpu.CompilerParams(dimension_semantics=sem, vmem_limit_bytes=VMEM_LIMIT_BYTES)


def _row_tile(m, preferred):
    t = preferred
    while m % t:
        t -= LANES
    return t


def _rms(x, g):
    return x * lax.rsqrt(jnp.mean(x * x, axis=-1, keepdims=True) + NORM_EPS) * g


def _dot(a, b):
    return jnp.dot(a, b, preferred_element_type=F32)


def _dot_nt(a, b):
    return lax.dot_general(a, b, (((1,), (1,)), ((), ())), preferred_element_type=F32)


def _dot_tn(a, b):
    return lax.dot_general(a, b, (((0,), (0,)), ((), ())), preferred_element_type=F32)


def _split3(x):
    hi = x.astype(BF16)
    r1 = x - hi.astype(F32)
    mid = r1.astype(BF16)
    lo = (r1 - mid.astype(F32)).astype(BF16)
    return hi, mid, lo


def _dot_sel(x, sel):
    hi, mid, lo = _split3(x)
    return _dot(hi, sel) + _dot(mid, sel) + _dot(lo, sel)


def _sel_dot(sel, x):
    hi, mid, lo = _split3(x)
    return _dot(sel, hi) + _dot(sel, mid) + _dot(sel, lo)


def _sel_dot_nt(x, sel):
    hi, mid, lo = _split3(x)
    return _dot_nt(hi, sel) + _dot_nt(mid, sel) + _dot_nt(lo, sel)


def _norm_kernel(x_ref, g_ref, o_ref):
    o_ref[...] = _rms(x_ref[...], g_ref[...]).astype(o_ref.dtype)


def _norm(x, g, out_dtype, tm=512):
    m, d = x.shape
    tm = _row_tile(m, tm)
    return pl.pallas_call(
        _norm_kernel,
        grid=(m // tm,),
        in_specs=[pl.BlockSpec((tm, d), lambda i: (i, 0)), pl.BlockSpec((1, d), lambda i: (0, 0))],
        out_specs=pl.BlockSpec((tm, d), lambda i: (i, 0)),
        out_shape=jax.ShapeDtypeStruct((m, d), out_dtype),
        compiler_params=_params("parallel"),
        name="rmsnorm",
    )(x, g.reshape(1, d))


def _proj_kernel(*refs, nx, nxv, nw, ne, prologue, epilogue):
    xs = refs[:nx]
    xvs = refs[nx:nx + nxv]
    ws = refs[nx + nxv:nx + nxv + nw]
    es = refs[nx + nxv + nw:nx + nxv + nw + ne]
    o_ref = refs[nx + nxv + nw + ne]
    xb_ref = refs[nx + nxv + nw + ne + 1]

    @pl.when(pl.program_id(1) == 0)
    def _():
        xb_ref[...] = prologue(*[r[...] for r in xs], *[r[...] for r in xvs]).astype(BF16)

    xb = xb_ref[...]
    accs = [_dot(xb, w[...]) for w in ws]
    o_ref[...] = epilogue(*accs, *[e[...] for e in es]).astype(o_ref.dtype)


def _proj(xs, ws, *, out_dtype, tm, tn, x_rowvecs=(), extras=(), prologue=lambda x: x,
          epilogue=lambda a: a, name="proj"):
    m, k = xs[0].shape
    n = ws[0].shape[1]
    tm = _row_tile(m, tm)
    kern = functools.partial(_proj_kernel, nx=len(xs), nxv=len(x_rowvecs), nw=len(ws), ne=len(extras),
                             prologue=prologue, epilogue=epilogue)
    in_specs = ([pl.BlockSpec((tm, k), lambda i, j: (i, 0)) for _ in xs]
                + [pl.BlockSpec((1, k), lambda i, j: (0, 0)) for _ in x_rowvecs]
                + [pl.BlockSpec((k, tn), lambda i, j: (0, j)) for _ in ws]
                + [pl.BlockSpec((1, tn), lambda i, j: (0, j)) for _ in extras])
    return pl.pallas_call(
        kern,
        grid=(m // tm, n // tn),
        in_specs=in_specs,
        out_specs=pl.BlockSpec((tm, tn), lambda i, j: (i, j)),
        out_shape=jax.ShapeDtypeStruct((m, n), out_dtype),
        scratch_shapes=[pltpu.VMEM((tm, k), BF16)],
        compiler_params=_params("parallel", "arbitrary"),
        name=name,
    )(*xs, *[v.reshape(1, k) for v in x_rowvecs], *ws, *[e.reshape(1, n) for e in extras])


def _lora_kernel(x_ref, xp_ref, mu_ref, w1_ref, w2_ref, b_ref, o_ref, *, mid, final):
    x = x_ref[...]
    xm = (x + (xp_ref[...] - x) * mu_ref[...]).astype(BF16)
    hid = mid(_dot(xm, w1_ref[...])).astype(BF16)
    o_ref[...] = final(b_ref[...] + _dot(hid, w2_ref[...])).astype(o_ref.dtype)


def _lora(x, xp, mu, w1, w2, bias, *, mid, final, out_dtype, tm=512, name="lora"):
    m, d = x.shape
    r = w1.shape[1]
    tm = _row_tile(m, tm)
    return pl.pallas_call(
        functools.partial(_lora_kernel, mid=mid, final=final),
        grid=(m // tm,),
        in_specs=[pl.BlockSpec((tm, d), lambda i: (i, 0)), pl.BlockSpec((tm, d), lambda i: (i, 0)),
                  pl.BlockSpec((1, d), lambda i: (0, 0)), pl.BlockSpec((d, r), lambda i: (0, 0)),
                  pl.BlockSpec((r, d), lambda i: (0, 0)), pl.BlockSpec((1, d), lambda i: (0, 0))],
        out_specs=pl.BlockSpec((tm, d), lambda i: (i, 0)),
        out_shape=jax.ShapeDtypeStruct((m, d), out_dtype),
        compiler_params=_params("parallel"),
        name=name,
    )(x, xp, mu.reshape(1, d), w1, w2, bias.reshape(1, d))


def _res_kernel(*refs, nk, gated, emit_h):
    if gated:
        x_ref, w_ref, res_ref, g_ref, x2_ref, w2_ref = refs[:6]
        rest = refs[6:]
    else:
        x_ref, w_ref, res_ref, g_ref = refs[:4]
        rest = refs[4:]
    if emit_h:
        h_ref, hn_ref, acc_ref = rest
    else:
        hn_ref, acc_ref = rest
    k = pl.program_id(1)

    @pl.when(k == 0)
    def _():
        acc_ref[...] = jnp.zeros_like(acc_ref)

    acc_ref[...] += _dot(x_ref[...], w_ref[...])

    @pl.when(k == nk - 1)
    def _():
        a = acc_ref[...]
        if gated:
            h = res_ref[...] + jax.nn.sigmoid(a) * _dot(x2_ref[...], w2_ref[...])
        else:
            h = res_ref[...] + a
        if emit_h:
            h_ref[...] = h
        hn_ref[...] = _rms(h, g_ref[...]).astype(hn_ref.dtype)


def _res(x, w, res, gain, *, tm, tk, hn_dtype, x2=None, w2=None, emit_h=True, name="res"):
    m, kdim = x.shape
    d = w.shape[1]
    tm = _row_tile(m, tm)
    nk = kdim // tk
    gated = x2 is not None
    in_specs = [pl.BlockSpec((tm, tk), lambda i, k: (i, k)), pl.BlockSpec((tk, d), lambda i, k: (k, 0)),
                pl.BlockSpec((tm, d), lambda i, k: (i, 0)), pl.BlockSpec((1, d), lambda i, k: (0, 0))]
    args = [x, w, res, gain.reshape(1, d)]
    if gated:
        k2 = x2.shape[1]
        in_specs += [pl.BlockSpec((tm, k2), lambda i, k: (i, 0)), pl.BlockSpec((k2, d), lambda i, k: (0, 0))]
        args += [x2, w2]
    out_specs = [pl.BlockSpec((tm, d), lambda i, k: (i, 0))]
    out_shape = [jax.ShapeDtypeStruct((m, d), hn_dtype)]
    if emit_h:
        out_specs = [pl.BlockSpec((tm, d), lambda i, k: (i, 0))] + out_specs
        out_shape = [jax.ShapeDtypeStruct((m, d), F32)] + out_shape
    return pl.pallas_call(
        functools.partial(_res_kernel, nk=nk, gated=gated, emit_h=emit_h),
        grid=(m // tm, nk),
        in_specs=in_specs,
        out_specs=out_specs,
        out_shape=out_shape,
        scratch_shapes=[pltpu.VMEM((tm, d), F32)],
        compiler_params=_params("parallel", "arbitrary"),
        name=name,
    )(*args)


def _mlstm_kernel(q_ref, k_ref, v_ref, og_ref, g_ref, gt_ref, nw_ref, c0_ref, n0_ref, m0_ref,
                  h_ref, c_ref, n_ref, m_ref, *, L):
    H, DK, DV = MLSTM_HEADS, MLSTM_DQK, MLSTM_DV

    @pl.when(pl.program_id(1) == 0)
    def _():
        c_ref[...] = c0_ref[...]
        n_ref[...] = n0_ref[...]
        m_ref[...] = m0_ref[...]

    q = q_ref[0]
    k = k_ref[0]
    v = v_ref[0]
    og = og_ref[0]
    G = g_ref[0]
    GT = gt_ref[0]
    nw = nw_ref[...]
    m_prev = m_ref[0]

    row = lax.broadcasted_iota(jnp.int32, (L, L), 0)
    col = lax.broadcasted_iota(jnp.int32, (L, L), 1)
    causal = col <= row
    tri = causal.astype(BF16)
    b_col = _sel_dot(tri, G)
    b_row = _sel_dot_nt(GT, tri)
    lane_h = lax.broadcasted_iota(jnp.int32, (1, H), 1)
    m_out = jnp.zeros((1, H), F32)

    for h in range(H):
        bc = b_col[:, H + h:H + h + 1]
        br = b_row[H + h:H + h + 1, :]
        li_r = GT[h:h + 1, :]
        li_c = G[:, h:h + 1]
        m_h = m_prev[:, h:h + 1]
        dlog = jnp.where(causal, bc - br + li_r, -jnp.inf)
        a = bc + m_h
        m_t = jnp.maximum(a, jnp.max(dlog, axis=1, keepdims=True))
        w_intra = jnp.exp(dlog - m_t)
        w_inter = jnp.exp(a - m_t)
        qh = q[:, h * DK:(h + 1) * DK]
        kh = k[:, h * DK:(h + 1) * DK]
        vh = v[:, h * DV:(h + 1) * DV]
        ch = c_ref[0, h]
        nh = n_ref[0, h:h + 1, :]
        s = _dot_nt(qh, kh) * w_intra
        num = w_inter * _dot(qh, ch.astype(BF16)) + _dot(s.astype(BF16), vh)
        den = (w_inter * jnp.sum(qh.astype(F32) * nh, axis=1, keepdims=True)
               + jnp.sum(s, axis=1, keepdims=True))
        hh = num / jnp.maximum(jnp.abs(den), jnp.exp(-m_t))
        hh = hh * lax.rsqrt(jnp.mean(hh * hh, axis=1, keepdims=True) + NORM_EPS)
        o = jax.nn.sigmoid(og[:, h * DV:(h + 1) * DV].astype(F32))
        h_ref[0, :, h * DV:(h + 1) * DV] = (o * (hh * nw[:, h * DV:(h + 1) * DV])).astype(h_ref.dtype)

        m_new = m_t[L - 1:L, :]
        b_last = bc[L - 1:L, :]
        decay = jnp.exp(b_last + m_h - m_new)
        wk = jnp.exp(b_last - bc + li_c - m_new)
        vw = (vh.astype(F32) * wk).astype(BF16)
        c_ref[0, h] = decay * ch + _dot_tn(kh, vw)
        n_ref[0, h:h + 1, :] = decay * nh + jnp.sum(kh.astype(F32) * wk, axis=0, keepdims=True)
        m_out = jnp.where(lane_h == h, m_new, m_out)

    m_ref[0] = m_out


def _mlstm(qkvo3, g3, gt3, nw, c0, n0, m0, *, n_seq, nc, L, row0):
    H, DK, DV = MLSTM_HEADS, MLSTM_DQK, MLSTM_DV
    nqk, nv = H * DK, H * DV

    def rows(colblk):
        return lambda s, c: (row0 + s * nc + c, 0, colblk)

    state = lambda s, c: (s, 0, 0, 0)
    state3 = lambda s, c: (s, 0, 0)
    in_specs = [
        pl.BlockSpec((1, L, nqk), rows(0)),
        pl.BlockSpec((1, L, nqk), rows(1)),
        pl.BlockSpec((1, L, nv), rows(1)),
        pl.BlockSpec((1, L, nv), rows(2)),
        pl.BlockSpec((1, L, LANES), rows(0)),
        pl.BlockSpec((1, LANES, L), rows(0)),
        pl.BlockSpec((1, nv), lambda s, c: (0, 0)),
        pl.BlockSpec((1, H, DK, DV), state),
        pl.BlockSpec((1, H, DK), state3),
        pl.BlockSpec((1, 1, H), state3),
    ]
    out_specs = [
        pl.BlockSpec((1, L, nv), lambda s, c: (s * nc + c, 0, 0)),
        pl.BlockSpec((1, H, DK, DV), state),
        pl.BlockSpec((1, H, DK), state3),
        pl.BlockSpec((1, 1, H), state3),
    ]
    out_shape = [
        jax.ShapeDtypeStruct((n_seq * nc, L, nv), BF16),
        jax.ShapeDtypeStruct((n_seq, H, DK, DV), F32),
        jax.ShapeDtypeStruct((n_seq, H, DK), F32),
        jax.ShapeDtypeStruct((n_seq, 1, H), F32),
    ]
    return pl.pallas_call(
        functools.partial(_mlstm_kernel, L=L),
        grid=(n_seq, nc),
        in_specs=in_specs,
        out_specs=out_specs,
        out_shape=out_shape,
        compiler_params=_params("parallel", "arbitrary"),
        name="mlstm_chunks",
    )(qkvo3, qkvo3, qkvo3, qkvo3, g3, gt3, nw.reshape(1, nv), c0, n0, m0.reshape(n_seq, 1, H))


def _rwkv_kernel(r_ref, k_ref, v_ref, lw_ref, a_ref, g_ref, kk_ref, ka_ref, rk_ref, lnw_ref, lnb_ref,
                 e_ref, et_ref, s0_ref, o_ref, s_ref, sbd_ref, y_ref, *, L, nc):
    N = RWKV_HEAD_DIM
    c = pl.program_id(1)
    lane = lax.broadcasted_iota(jnp.int32, (1, LANES), 1)
    head0 = lane < N

    @pl.when(c == 0)
    def _():
        z = jnp.zeros((N, N), F32)
        for p in range(RWKV_PAIRS):
            top = jnp.concatenate([s0_ref[0, 2 * p], z], axis=1)
            bot = jnp.concatenate([z, s0_ref[0, 2 * p + 1]], axis=1)
            sbd_ref[p] = jnp.concatenate([top, bot], axis=0)

    E = e_ref[...]
    ET = et_ref[...]

    def seg_sum_bcast(x):
        return _dot_sel(_dot_sel(x, E), ET)

    r = r_ref[0].astype(F32)
    k = k_ref[0].astype(F32)
    v = v_ref[0].astype(F32)
    lw = lw_ref[0]
    a = a_ref[0]
    kk = k * kk_ref[...]
    kk = kk / jnp.maximum(jnp.sqrt(seg_sum_bcast(kk * kk)), L2_EPS)
    k2 = k * (1.0 + (a - 1.0) * ka_ref[...])

    row = lax.broadcasted_iota(jnp.int32, (L, L), 0)
    col = lax.broadcasted_iota(jnp.int32, (L, L), 1)
    tri = (col <= row).astype(BF16)
    logp = _sel_dot(tri, lw)
    p_incl = jnp.exp(logp)
    inv_p = jnp.exp(-logp)
    at = -kk * jnp.exp(logp - lw)
    bt = kk * a * inv_p
    kt = k2 * inv_p
    rt = r * p_incl
    p_last = p_incl[L - 1:L, :]

    row2 = lax.broadcasted_iota(jnp.int32, (2 * L, 2 * L), 0)
    col2 = lax.broadcasted_iota(jnp.int32, (2 * L, 2 * L), 1)
    same = (row2 >= L) == (col2 >= L)
    strict = same & (col2 < row2)
    incl = same & (col2 <= row2)
    eye = (row2 == col2).astype(F32)
    n_sq = max(int(math.ceil(math.log2(L))) - 1, 0)

    def stack(x):
        return jnp.concatenate([jnp.where(head0, x, 0.0), jnp.where(head0, 0.0, x)], axis=0)

    for p in range(RWKV_PAIRS):
        sl = slice(p * LANES, (p + 1) * LANES)
        xa = stack(at[:, sl]).astype(BF16)
        xr = stack(rt[:, sl]).astype(BF16)
        yb = stack(bt[:, sl]).astype(BF16)
        yk = stack(kt[:, sl]).astype(BF16)
        vr = stack(v[:, sl]).astype(BF16)
        sbd = sbd_ref[p]
        sb = sbd.astype(BF16)

        n_ab = jnp.where(strict, _dot_nt(xa, yb), 0.0)
        a_ak = jnp.where(strict, _dot_nt(xa, yk), 0.0).astype(BF16)
        a_rb = jnp.where(incl, _dot_nt(xr, yb), 0.0).astype(BF16)
        a_rk = jnp.where(incl, _dot_nt(xr, yk), 0.0).astype(BF16)

        t_inv = eye + n_ab
        pw = n_ab
        for _ in range(n_sq):
            pwb = pw.astype(BF16)
            pw = _dot(pwb, pwb)
            t_inv = t_inv + _dot(t_inv.astype(BF16), pw.astype(BF16))

        u = _dot(t_inv.astype(BF16), (_dot_nt(xa, sb) + _dot(a_ak, vr)).astype(BF16))
        ub = u.astype(BF16)
        yr = _dot_nt(xr, sb) + _dot(a_rb, ub) + _dot(a_rk, vr)
        y_ref[:, sl] = yr[:L] + yr[L:]
        s_new = (sbd + _dot_tn(ub, yb) + _dot_tn(vr, yk)) * p_last[:, sl]
        sbd_ref[p] = s_new

    y = y_ref[...]
    mean = seg_sum_bcast(y) * (1.0 / N)
    yc = y - mean
    var = seg_sum_bcast(yc * yc) * (1.0 / N)
    yn = yc * lax.rsqrt(var + RWKV_GN_EPS) * lnw_ref[...] + lnb_ref[...]
    bonus = seg_sum_bcast(r * k2 * rk_ref[...]) * v
    o_ref[0] = ((yn + bonus) * g_ref[0].astype(F32)).astype(o_ref.dtype)

    @pl.when(c == nc - 1)
    def _():
        for p in range(RWKV_PAIRS):
            sbd = sbd_ref[p]
            s_ref[0, 2 * p] = sbd[:N, :N]
            s_ref[0, 2 * p + 1] = sbd[N:, N:]


def _rwkv(rkv3, lw3, a3, g3, k_k, k_a, r_k, ln_w, ln_b, s0, *, n_seq, nc, L, row0):
    D, H, N = D_MODEL, RWKV_HEADS, RWKV_HEAD_DIM
    onehot = (jnp.arange(D)[:, None] // N == jnp.arange(H)[None, :]).astype(BF16)

    def rows(colblk):
        return lambda s, c: (row0 + s * nc + c, 0, colblk)

    vec = pl.BlockSpec((1, D), lambda s, c: (0, 0))
    state = lambda s, c: (s, 0, 0, 0)
    in_specs = [
        pl.BlockSpec((1, L, D), rows(0)), pl.BlockSpec((1, L, D), rows(1)), pl.BlockSpec((1, L, D), rows(2)),
        pl.BlockSpec((1, L, D), rows(0)), pl.BlockSpec((1, L, D), rows(0)), pl.BlockSpec((1, L, D), rows(0)),
        vec, vec, vec, vec, vec,
        pl.BlockSpec((D, H), lambda s, c: (0, 0)), pl.BlockSpec((H, D), lambda s, c: (0, 0)),
        pl.BlockSpec((1, H, N, N), state),
    ]
    out_specs = [
        pl.BlockSpec((1, L, D), lambda s, c: (s * nc + c, 0, 0)),
        pl.BlockSpec((1, H, N, N), state),
    ]
    out_shape = [
        jax.ShapeDtypeStruct((n_seq * nc, L, D), BF16),
        jax.ShapeDtypeStruct((n_seq, H, N, N), F32),
    ]
    return pl.pallas_call(
        functools.partial(_rwkv_kernel, L=L, nc=nc),
        grid=(n_seq, nc),
        in_specs=in_specs,
        out_specs=out_specs,
        out_shape=out_shape,
        scratch_shapes=[pltpu.VMEM((RWKV_PAIRS, LANES, LANES), F32), pltpu.VMEM((L, D), F32)],
        compiler_params=_params("parallel", "arbitrary"),
        name="rwkv7_chunks",
    )(rkv3, rkv3, rkv3, lw3, a3, g3, k_k.reshape(1, D), k_a.reshape(1, D), r_k.reshape(1, D),
      ln_w.reshape(1, D), ln_b.reshape(1, D), onehot, onehot.T, s0)


def _softcap(x):
    return GATE_SOFTCAP * jnp.tanh(x / GATE_SOFTCAP)


def _mlstm_gate_epilogue(acc, bias):
    z = _softcap(acc + bias)
    lane = lax.broadcasted_iota(jnp.int32, z.shape, 1)
    return jnp.where(lane < MLSTM_HEADS, z, jax.nn.log_sigmoid(z))


def _mix(x, xp, mu):
    return x + (xp - x) * mu


def _log_decay(z):
    return -jnp.exp(-jax.nn.softplus(-z) - 0.5)


def _ffn_ple(h, hn, p_all, i, w, next_gain, last):
    act = _proj([hn], [w['ffn_w_gate'][i].astype(BF16), w['ffn_w_up'][i].astype(BF16)],
                out_dtype=BF16, tm=1024, tn=512, epilogue=lambda g, u: jax.nn.silu(g) * u, name="swiglu_up")
    h, hn = _res(act, w['ffn_w_down'][i].astype(BF16), h, w['norm_ple'][i], tm=512, tk=512,
                 hn_dtype=BF16, name="swiglu_down")
    outs = _res(hn, w['ple_w_gate'][i].astype(BF16), h, next_gain, tm=512, tk=512, hn_dtype=F32,
                x2=p_all[i].astype(BF16), w2=w['ple_w_proj'][i].astype(BF16), emit_h=not last, name="ple")
    return outs


def kernel(x_prompt, x_sample, state_mlstm_C, state_mlstm_n, state_mlstm_m, state_rwkv_S, state_rwkv_shift,
           p_prompt, p_sample, norm_mix, norm_ffn, norm_ple, norm_final, ffn_w_gate, ffn_w_up, ffn_w_down,
           ple_w_proj, ple_w_gate, mlstm_w_q, mlstm_w_k, mlstm_w_v, mlstm_w_igate, mlstm_b_igate,
           mlstm_w_fgate, mlstm_b_fgate, mlstm_w_ogate, mlstm_norm_w, mlstm_w_out, rwkv_mu, rwkv_w_r,
           rwkv_w_k, rwkv_w_v, rwkv_w_o, rwkv_w0, rwkv_w1, rwkv_w2, rwkv_a0, rwkv_a1, rwkv_a2, rwkv_g1,
           rwkv_g2, rwkv_k_k, rwkv_k_a, rwkv_r_k, rwkv_ln_w, rwkv_ln_b):
    w = dict(ffn_w_gate=ffn_w_gate, ffn_w_up=ffn_w_up, ffn_w_down=ffn_w_down, ple_w_proj=ple_w_proj,
             ple_w_gate=ple_w_gate, norm_ple=norm_ple)
    D = D_MODEL
    B, T, _ = x_prompt.shape
    BS, TS, _ = x_sample.shape
    MP, MS = B * T, BS * TS
    M = MP + MS
    H, DK, DV = MLSTM_HEADS, MLSTM_DQK, MLSTM_DV
    RH, RN = RWKV_HEADS, RWKV_HEAD_DIM

    h = jnp.concatenate([x_prompt.reshape(MP, D), x_sample.reshape(MS, D)], axis=0)
    p_all = jnp.concatenate([p_prompt.reshape(2, MP, -1), p_sample.reshape(2, MS, -1)], axis=1)

    hn = _norm(h, norm_mix[0], BF16)
    w_qkvo = jnp.concatenate([mlstm_w_q[0], mlstm_w_k[0], mlstm_w_v[0], mlstm_w_ogate[0]], axis=1).astype(BF16)
    col_scale = jnp.concatenate([jnp.ones((H * DK,), F32), jnp.full((H * DK,), DK ** -0.5, F32),
                                 jnp.ones((2 * H * DV,), F32)])
    qkvo = _proj([hn], [w_qkvo], out_dtype=BF16, tm=1024, tn=1024, extras=[col_scale],
                 epilogue=lambda a, s: a * s, name="mlstm_qkvo")
    w_gates = jnp.zeros((D, LANES), F32).at[:, :H].set(mlstm_w_igate[0]).at[:, H:2 * H].set(mlstm_w_fgate[0])
    b_gates = jnp.zeros((LANES,), F32).at[:H].set(mlstm_b_igate[0]).at[H:2 * H].set(mlstm_b_fgate[0])
    gates = _proj([hn], [w_gates.astype(BF16)], out_dtype=F32, tm=1024, tn=LANES, extras=[b_gates],
                  epilogue=_mlstm_gate_epilogue, name="mlstm_gates")

    LP = math.gcd(T, MLSTM_CHUNK)
    LS = math.gcd(TS, MLSTM_CHUNK)
    ncp, ncs = T // LP, TS // LS
    gp = gates.reshape(M // LP, LP, LANES)
    gs = gates.reshape(M // LS, LS, LANES)
    zC = jnp.zeros((B, H, DK, DV), F32)
    zn = jnp.zeros((B, H, DK), F32)
    zm = jnp.zeros((B, H), F32)
    hm_p, C_p, n_p, m_p = _mlstm(qkvo.reshape(M // LP, LP, -1), gp, gp.transpose(0, 2, 1), mlstm_norm_w[0],
                                 zC, zn, zm, n_seq=B, nc=ncp, L=LP, row0=0)
    hm_s, C_s, n_s, m_s = _mlstm(qkvo.reshape(M // LS, LS, -1), gs, gs.transpose(0, 2, 1), mlstm_norm_w[0],
                                 state_mlstm_C[0], state_mlstm_n[0], state_mlstm_m[0],
                                 n_seq=BS, nc=ncs, L=LS, row0=MP // LS)
    hmix = jnp.concatenate([hm_p.reshape(MP, D), hm_s.reshape(MS, D)], axis=0)
    h, hn = _res(hmix, mlstm_w_out[0].astype(BF16), h, norm_ffn[0], tm=512, tk=512, hn_dtype=BF16,
                 name="mlstm_out")
    h, xn = _ffn_ple(h, hn, p_all, 0, w, norm_mix[1], last=False)

    xn_p = xn[:MP].reshape(B, T, D)
    xn_s = xn[MP:].reshape(BS, TS, D)
    shift_p = xn_p[:, -1]
    shift_s = xn_s[:, -1]
    xprev = jnp.concatenate([
        jnp.concatenate([jnp.zeros((B, 1, D), F32), xn_p[:, :-1]], axis=1).reshape(MP, D),
        jnp.concatenate([state_rwkv_shift[0][:, None, :], xn_s[:, :-1]], axis=1).reshape(MS, D)], axis=0)
    mu = rwkv_mu[0]
    rkv = [_proj([xn, xprev], [wm.astype(BF16)], out_dtype=BF16, tm=512, tn=1024, x_rowvecs=[mu[c]],
                 prologue=_mix, name=nm)
           for wm, c, nm in ((rwkv_w_r[0], 0, "rwkv_r"), (rwkv_w_k[0], 2, "rwkv_k"), (rwkv_w_v[0], 3, "rwkv_v"))]
    rkv = jnp.concatenate(rkv, axis=1)

    def pad_rank(w1, w2):
        r = w1.shape[1]
        rp = -(-r // LANES) * LANES
        return (jnp.pad(w1, ((0, 0), (0, rp - r))).astype(BF16), jnp.pad(w2, ((0, rp - r), (0, 0))).astype(BF16))

    ident = lambda z: z
    lw = _lora(xn, xprev, mu[1], *pad_rank(rwkv_w1[0], rwkv_w2[0]), rwkv_w0[0], mid=jnp.tanh, final=_log_decay,
               out_dtype=F32, name="rwkv_decay")
    aa = _lora(xn, xprev, mu[4], *pad_rank(rwkv_a1[0], rwkv_a2[0]), rwkv_a0[0], mid=ident, final=jax.nn.sigmoid,
               out_dtype=F32, name="rwkv_a")
    gg = _lora(xn, xprev, mu[5], *pad_rank(rwkv_g1[0], rwkv_g2[0]), jnp.zeros((D,), F32), mid=jax.nn.sigmoid,
               final=ident, out_dtype=BF16, name="rwkv_g")

    LRP = math.gcd(T, RWKV_CHUNK)
    LRS = math.gcd(TS, RWKV_CHUNK)

    def run_rwkv(L, n_seq, t_len, row0, s0):
        return _rwkv(rkv.reshape(M // L, L, 3 * D), lw.reshape(M // L, L, D), aa.reshape(M // L, L, D),
                     gg.reshape(M // L, L, D), rwkv_k_k[0], rwkv_k_a[0], rwkv_r_k[0], rwkv_ln_w[0], rwkv_ln_b[0],
                     s0, n_seq=n_seq, nc=t_len // L, L=L, row0=row0)

    y_p, S_p = run_rwkv(LRP, B, T, 0, jnp.zeros((B, RH, RN, RN), F32))
    y_s, S_s = run_rwkv(LRS, BS, TS, MP // LRS, state_rwkv_S[0])
    ymix = jnp.concatenate([y_p.reshape(MP, D), y_s.reshape(MS, D)], axis=0)
    h, hn = _res(ymix, rwkv_w_o[0].astype(BF16), h, norm_ffn[1], tm=512, tk=512, hn_dtype=BF16, name="rwkv_out")
    (y,) = _ffn_ple(h, hn, p_all, 1, w, norm_final, last=True)

    return (y[:MP].reshape(B, T, D), y[MP:].reshape(BS, TS, D),
            C_p[None], n_p[None], m_p.reshape(1, B, H), S_p[None], shift_p[None],
            C_s[None], n_s[None], m_s.reshape(1, BS, H), S_s[None], shift_s[None])
```

```python
import functools
import math

import jax
import jax.numpy as jnp
from jax import lax
from jax.experimental import pallas as pl
from jax.experimental.pallas import tpu as pltpu

F32 = jnp.float32
BF16 = jnp.bfloat16

D_MODEL = 2048
MLSTM_HEADS = 8
MLSTM_DQK = 128
MLSTM_DV = 256
MLSTM_CHUNK = 64
GATE_SOFTCAP = 15.0
RWKV_HEAD_DIM = 64
RWKV_HEADS = 32
RWKV_PAIRS = RWKV_HEADS // 2
RWKV_CHUNK = 64
NORM_EPS = 1e-6
RWKV_GN_EPS = 64e-5
L2_EPS = 1e-12
LANES = 128
SUBLANES = 8
VMEM_LIMIT_BYTES = 56 * 1024 * 1024


def _params(*sem):
    return pltpu.CompilerParams(dimension_semantics=sem, vmem_limit_bytes=VMEM_LIMIT_BYTES)


def _row_tile(m, preferred):
    t = preferred
    while m % t:
        t -= LANES
    return t


def _rms(x, g):
    return x * lax.rsqrt(jnp.mean(x * x, axis=-1, keepdims=True) + NORM_EPS) * g


def _dot(a, b):
    return jnp.dot(a, b, preferred_element_type=F32)


def _dot_nt(a, b):
    return lax.dot_general(a, b, (((1,), (1,)), ((), ())), preferred_element_type=F32)


def _dot_tn(a, b):
    return lax.dot_general(a, b, (((0,), (0,)), ((), ())), preferred_element_type=F32)


def _split3(x):
    hi = x.astype(BF16)
    r1 = x - hi.astype(F32)
    mid = r1.astype(BF16)
    lo = (r1 - mid.astype(F32)).astype(BF16)
    return hi, mid, lo


def _dot_sel(x, sel):
    hi, mid, lo = _split3(x)
    return _dot(hi, sel) + _dot(mid, sel) + _dot(lo, sel)


def _sel_dot(sel, x):
    hi, mid, lo = _split3(x)
    return _dot(sel, hi) + _dot(sel, mid) + _dot(sel, lo)


def _sel_dot_nt(x, sel):
    hi, mid, lo = _split3(x)
    return _dot_nt(hi, sel) + _dot_nt(mid, sel) + _dot_nt(lo, sel)


def _row_specs(x, tm, cols, col_index):
    if isinstance(x, tuple):
        a, b = x
        na = a.shape[0] // tm
        return ([pl.BlockSpec((tm, cols), lambda i, *r: (jnp.minimum(i, na - 1), col_index(*r))),
                 pl.BlockSpec((tm, cols), lambda i, *r: (jnp.maximum(i - na, 0), col_index(*r)))], [a, b], na)
    return [pl.BlockSpec((tm, cols), lambda i, *r: (i, col_index(*r)))], [x], None


def _rows(x):
    return sum(a.shape[0] for a in x) if isinstance(x, tuple) else x.shape[0]


def _pair_row_tile(operands, preferred, extra_rows=()):
    halves = [a.shape[0] for x in operands if x is not None for a in (x if isinstance(x, tuple) else (x,))]
    return _row_tile(math.gcd(*halves, *extra_rows), preferred)


def _load_rows(refs, na):
    if na is None:
        return refs[0][...]
    return jnp.where(pl.program_id(0) < na, refs[0][...], refs[1][...])


def _store_rows(refs, na, val):
    if na is None:
        refs[0][...] = val.astype(refs[0].dtype)
        return

    @pl.when(pl.program_id(0) < na)
    def _():
        refs[0][...] = val.astype(refs[0].dtype)

    @pl.when(pl.program_id(0) >= na)
    def _():
        refs[1][...] = val.astype(refs[1].dtype)


_zero_col = lambda *r: 0


def _norm_kernel(*refs, na):
    nx = 1 if na is None else 2
    g_ref, o_ref = refs[nx], refs[nx + 1]
    o_ref[...] = _rms(_load_rows(refs[:nx], na), g_ref[...]).astype(o_ref.dtype)


def _norm(x, g, out_dtype, tm=512):
    m = _rows(x)
    d = g.shape[0]
    tm = _pair_row_tile([x], tm)
    x_specs, x_args, na = _row_specs(x, tm, d, _zero_col)
    return pl.pallas_call(
        functools.partial(_norm_kernel, na=na),
        grid=(m // tm,),
        in_specs=x_specs + [pl.BlockSpec((1, d), lambda i: (0, 0))],
        out_specs=pl.BlockSpec((tm, d), lambda i: (i, 0)),
        out_shape=jax.ShapeDtypeStruct((m, d), out_dtype),
        compiler_params=_params("parallel"),
        name="rmsnorm",
    )(*x_args, g.reshape(1, d))


def _proj_kernel(*refs, nw, ne, epilogue):
    x_ref = refs[0]
    ws = refs[1:1 + nw]
    es = refs[1 + nw:1 + nw + ne]
    o_ref = refs[1 + nw + ne]
    xb = x_ref[...]
    accs = [_dot(xb, w[...]) for w in ws]
    o_ref[...] = epilogue(*accs, *[e[...] for e in es]).astype(o_ref.dtype)


def _proj(x, ws, *, out_dtype, tm, tn, extras=(), epilogue=lambda a: a, name="proj"):
    m, k = x.shape
    n = ws[0].shape[1]
    tm = _row_tile(m, tm)
    in_specs = ([pl.BlockSpec((tm, k), lambda i, j: (i, 0))]
                + [pl.BlockSpec((k, tn), lambda i, j: (0, j)) for _ in ws]
                + [pl.BlockSpec((1, tn), lambda i, j: (0, j)) for _ in extras])
    return pl.pallas_call(
        functools.partial(_proj_kernel, nw=len(ws), ne=len(extras), epilogue=epilogue),
        grid=(m // tm, n // tn),
        in_specs=in_specs,
        out_specs=pl.BlockSpec((tm, tn), lambda i, j: (i, j)),
        out_shape=jax.ShapeDtypeStruct((m, n), out_dtype),
        compiler_params=_params("parallel", "arbitrary"),
        name=name,
    )(x, *ws, *[e.reshape(1, n) for e in extras])


def _res_kernel(*refs, nk, gated, emit_h, na_x, na_res, na_x2, na_out):
    pos = [0]

    def take(n):
        out = refs[pos[0]:pos[0] + n]
        pos[0] += n
        return out

    width = lambda na: 1 if na is None else 2
    x_refs = take(width(na_x))
    (w_ref,) = take(1)
    res_refs = take(width(na_res))
    (g_ref,) = take(1)
    if gated:
        x2_refs = take(width(na_x2))
        (w2_ref,) = take(1)
    h_refs = take(1) if emit_h else ()
    hn_refs = take(width(na_out))
    acc_refs = take(1) if nk > 1 else ()
    k = pl.program_id(1)

    part = _dot(_load_rows(x_refs, na_x), w_ref[...])
    if nk > 1:
        acc_ref = acc_refs[0]

        @pl.when(k == 0)
        def _():
            acc_ref[...] = part

        @pl.when(k > 0)
        def _():
            acc_ref[...] += part

    @pl.when(k == nk - 1)
    def _():
        a = acc_refs[0][...] if nk > 1 else part
        res = _load_rows(res_refs, na_res)
        if gated:
            h = res + jax.nn.sigmoid(a) * _dot(_load_rows(x2_refs, na_x2).astype(BF16), w2_ref[...])
        else:
            h = res + a
        if emit_h:
            h_refs[0][...] = h
        _store_rows(hn_refs, na_out, _rms(h, g_ref[...]))


def _res(x, w, res, gain, *, tm, tk, hn_dtype, x2=None, w2=None, emit_h=True, split_out=None, name="res"):
    m = _rows(x)
    kdim, d = w.shape
    tm = _pair_row_tile([x, res, x2], tm, () if split_out is None else (split_out, m - split_out))
    nk = kdim // tk
    gated = x2 is not None
    x_specs, x_args, na_x = _row_specs(x, tm, tk, lambda k: k)
    res_specs, res_args, na_res = _row_specs(res, tm, d, _zero_col)
    const = lambda i, k: (0, 0)
    w_mode = dict(pipeline_mode=pl.Buffered(1)) if nk == 1 else {}
    in_specs = (x_specs + [pl.BlockSpec((tk, d), lambda i, k: (k, 0), **w_mode)]
                + res_specs + [pl.BlockSpec((1, d), const)])
    args = x_args + [w] + res_args + [gain.reshape(1, d)]
    na_x2 = None
    if gated:
        k2 = w2.shape[0]
        x2_specs, x2_args, na_x2 = _row_specs(x2, tm, k2, _zero_col)
        in_specs += x2_specs + [pl.BlockSpec((k2, d), const, pipeline_mode=pl.Buffered(1))]
        args += x2_args + [w2]
    out_specs, out_shape = [], []
    if emit_h:
        out_specs.append(pl.BlockSpec((tm, d), lambda i, k: (i, 0)))
        out_shape.append(jax.ShapeDtypeStruct((m, d), F32))
    na_out = None
    if split_out is None:
        out_specs.append(pl.BlockSpec((tm, d), lambda i, k: (i, 0)))
        out_shape.append(jax.ShapeDtypeStruct((m, d), hn_dtype))
    else:
        na_out = split_out // tm
        out_specs += [pl.BlockSpec((tm, d), lambda i, k: (jnp.minimum(i, na_out - 1), 0)),
                      pl.BlockSpec((tm, d), lambda i, k: (jnp.maximum(i - na_out, 0), 0))]
        out_shape += [jax.ShapeDtypeStruct((split_out, d), hn_dtype),
                      jax.ShapeDtypeStruct((m - split_out, d), hn_dtype)]
    return pl.pallas_call(
        functools.partial(_res_kernel, nk=nk, gated=gated, emit_h=emit_h, na_x=na_x, na_res=na_res,
                          na_x2=na_x2, na_out=na_out),
        grid=(m // tm, nk),
        in_specs=in_specs,
        out_specs=out_specs,
        out_shape=out_shape,
        scratch_shapes=[pltpu.VMEM((tm, d), F32)] if nk > 1 else [],
        compiler_params=_params("parallel", "arbitrary"),
        name=name,
    )(*args)


def _mlstm_kernel(q_ref, k_ref, v_ref, og_ref, g_ref, gt_ref, nw_ref, c0_ref, n0_ref, m0_ref,
                  h_ref, c_ref, n_ref, m_ref, *, L):
    H, DK, DV = MLSTM_HEADS, MLSTM_DQK, MLSTM_DV

    @pl.when(pl.program_id(1) == 0)
    def _():
        c_ref[...] = c0_ref[...]
        n_ref[...] = n0_ref[...]
        m_ref[...] = m0_ref[...]

    q = q_ref[0]
    k = k_ref[0]
    v = v_ref[0]
    og = og_ref[0]
    G = g_ref[0]
    GT = gt_ref[0]
    nw = nw_ref[...]
    m_prev = m_ref[0]

    row = lax.broadcasted_iota(jnp.int32, (L, L), 0)
    col = lax.broadcasted_iota(jnp.int32, (L, L), 1)
    causal = col <= row
    tri = causal.astype(BF16)
    b_col = _sel_dot(tri, G)
    b_row = _sel_dot_nt(GT, tri)
    lane_h = lax.broadcasted_iota(jnp.int32, (1, H), 1)

    hs = range(H)
    qh = [q[:, h * DK:(h + 1) * DK] for h in hs]
    kh = [k[:, h * DK:(h + 1) * DK] for h in hs]
    vh = [v[:, h * DV:(h + 1) * DV] for h in hs]
    ch = [c_ref[0, h] for h in hs]
    nh = [n_ref[0, h:h + 1, :] for h in hs]
    bc = [b_col[:, H + h:H + h + 1] for h in hs]
    li_c = [G[:, h:h + 1] for h in hs]
    m_h = [m_prev[:, h:h + 1] for h in hs]
    qk = [_dot_nt(qh[h], kh[h]) for h in hs]
    qc = [_dot(qh[h], ch[h].astype(BF16)) for h in hs]
    dlog = [jnp.where(causal, bc[h] - b_row[H + h:H + h + 1, :] + GT[h:h + 1, :], -jnp.inf) for h in hs]
    a = [bc[h] + m_h[h] for h in hs]
    m_t = [jnp.maximum(a[h], jnp.max(dlog[h], axis=1, keepdims=True)) for h in hs]
    w_inter = [jnp.exp(a[h] - m_t[h]) for h in hs]
    s = [qk[h] * jnp.exp(dlog[h] - m_t[h]) for h in hs]
    sv = [_dot(s[h].astype(BF16), vh[h]) for h in hs]
    m_new = [m_t[h][L - 1:L, :] for h in hs]
    b_last = [bc[h][L - 1:L, :] for h in hs]
    wk = [jnp.exp(b_last[h] - bc[h] + li_c[h] - m_new[h]) for h in hs]
    kv = [_dot_tn(kh[h], (vh[h].astype(F32) * wk[h]).astype(BF16)) for h in hs]
    decay = [jnp.exp(b_last[h] + m_h[h] - m_new[h]) for h in hs]
    for h in hs:
        c_ref[0, h] = decay[h] * ch[h] + kv[h]
        n_ref[0, h:h + 1, :] = decay[h] * nh[h] + jnp.sum(kh[h].astype(F32) * wk[h], axis=0, keepdims=True)
    m_out = jnp.zeros((1, H), F32)
    for h in hs:
        m_out = jnp.where(lane_h == h, m_new[h], m_out)
    m_ref[0] = m_out

    for h in hs:
        num = w_inter[h] * qc[h] + sv[h]
        den = (w_inter[h] * jnp.sum(qh[h].astype(F32) * nh[h], axis=1, keepdims=True)
               + jnp.sum(s[h], axis=1, keepdims=True))
        hh = num / jnp.maximum(jnp.abs(den), jnp.exp(-m_t[h]))
        hh = hh * lax.rsqrt(jnp.mean(hh * hh, axis=1, keepdims=True) + NORM_EPS)
        o = jax.nn.sigmoid(og[:, h * DV:(h + 1) * DV].astype(F32))
        h_ref[0, :, h * DV:(h + 1) * DV] = (o * (hh * nw[:, h * DV:(h + 1) * DV])).astype(h_ref.dtype)


def _mlstm(qkvo3, g3, gt3, nw, c0, n0, m0, *, n_seq, nc, L, row0):
    H, DK, DV = MLSTM_HEADS, MLSTM_DQK, MLSTM_DV
    nqk, nv = H * DK, H * DV

    def rows(colblk):
        return lambda s, c: (row0 + s * nc + c, 0, colblk)

    state = lambda s, c: (s, 0, 0, 0)
    state3 = lambda s, c: (s, 0, 0)
    in_specs = [
        pl.BlockSpec((1, L, nqk), rows(0)),
        pl.BlockSpec((1, L, nqk), rows(1)),
        pl.BlockSpec((1, L, nv), rows(1)),
        pl.BlockSpec((1, L, nv), rows(2)),
        pl.BlockSpec((1, L, LANES), rows(0)),
        pl.BlockSpec((1, LANES, L), rows(0)),
        pl.BlockSpec((1, nv), lambda s, c: (0, 0)),
        pl.BlockSpec((1, H, DK, DV), state),
        pl.BlockSpec((1, H, DK), state3),
        pl.BlockSpec((1, 1, H), state3),
    ]
    out_specs = [
        pl.BlockSpec((1, L, nv), lambda s, c: (s * nc + c, 0, 0)),
        pl.BlockSpec((1, H, DK, DV), state),
        pl.BlockSpec((1, H, DK), state3),
        pl.BlockSpec((1, 1, H), state3),
    ]
    out_shape = [
        jax.ShapeDtypeStruct((n_seq * nc, L, nv), BF16),
        jax.ShapeDtypeStruct((n_seq, H, DK, DV), F32),
        jax.ShapeDtypeStruct((n_seq, H, DK), F32),
        jax.ShapeDtypeStruct((n_seq, 1, H), F32),
    ]
    return pl.pallas_call(
        functools.partial(_mlstm_kernel, L=L),
        grid=(n_seq, nc),
        in_specs=in_specs,
        out_specs=out_specs,
        out_shape=out_shape,
        compiler_params=_params("parallel", "arbitrary"),
        name="mlstm_chunks",
    )(qkvo3, qkvo3, qkvo3, qkvo3, g3, gt3, nw.reshape(1, nv), c0, n0, m0.reshape(n_seq, 1, H))


def _shifted(x_ref, prev_ref, sx_ref, *, tm, t_prompt, t_sample, n_prompt_tiles):
    i = pl.program_id(0)
    x = x_ref[...]
    rolled = pltpu.roll(x, 1, axis=0)
    rowi = lax.broadcasted_iota(jnp.int32, (tm, 1), 0)
    first = jnp.where((i * tm) % t_prompt == 0, 0.0, prev_ref[SUBLANES - 1:SUBLANES, :])
    xprev = jnp.where(rowi == 0, first, rolled)
    at_sample_start = jnp.logical_and(i >= n_prompt_tiles, rowi % t_sample == 0)
    return x, jnp.where(at_sample_start, sx_ref[...], xprev)


def _shift_specs(tm, d, n_prompt_tiles):
    return [pl.BlockSpec((tm, d), lambda i, *r: (i, 0)),
            pl.BlockSpec((SUBLANES, d), lambda i, *r: (jnp.maximum(i * (tm // SUBLANES) - 1, 0), 0)),
            pl.BlockSpec((tm, d), lambda i, *r: (jnp.maximum(i - n_prompt_tiles, 0), 0))]


def _rkv_kernel(x_ref, prev_ref, sx_ref, mu_ref, w_ref, o_ref, xb_ref, *, tiles_per_group, shift):
    j = pl.program_id(1)

    @pl.when(j == 0)
    def _():
        x, xp = _shifted(x_ref, prev_ref, sx_ref, **shift)
        dx = xp - x
        for c in range(3):
            xb_ref[c] = (x + dx * mu_ref[c:c + 1, :]).astype(BF16)

    o_ref[...] = _dot(xb_ref[j // tiles_per_group], w_ref[...]).astype(o_ref.dtype)


def _rkv(xn, sx, mu3, w_rkv, *, tm, tn, shift):
    m, d = xn.shape
    n = w_rkv.shape[1]
    return pl.pallas_call(
        functools.partial(_rkv_kernel, tiles_per_group=d // tn, shift=shift),
        grid=(m // tm, n // tn),
        in_specs=_shift_specs(tm, d, shift["n_prompt_tiles"]) + [
            pl.BlockSpec((3, d), lambda i, j: (0, 0)), pl.BlockSpec((d, tn), lambda i, j: (0, j))],
        out_specs=pl.BlockSpec((tm, tn), lambda i, j: (i, j)),
        out_shape=jax.ShapeDtypeStruct((m, n), BF16),
        scratch_shapes=[pltpu.VMEM((3, tm, d), BF16)],
        compiler_params=_params("parallel", "arbitrary"),
        name="rwkv_rkv",
    )(xn, xn, sx, mu3, w_rkv)


def _lora_kernel(x_ref, prev_ref, sx_ref, mu_ref, w1_ref, w2_ref, b_ref, lw_ref, a_ref, g_ref, *, shift):
    x, xp = _shifted(x_ref, prev_ref, sx_ref, **shift)
    dx = xp - x

    def branch(c, mid):
        xm = (x + dx * mu_ref[c:c + 1, :]).astype(BF16)
        hid = mid(_dot(xm, w1_ref[c])).astype(BF16)
        return b_ref[c:c + 1, :] + _dot(hid, w2_ref[c])

    lw_ref[...] = -jnp.exp(-jax.nn.softplus(-branch(0, jnp.tanh)) - 0.5)
    a_ref[...] = jax.nn.sigmoid(branch(1, lambda z: z))
    g_ref[...] = branch(2, jax.nn.sigmoid).astype(g_ref.dtype)


def _lora(xn, sx, mu3, w1s, w2s, bias3, *, tm, shift):
    m, d = xn.shape
    r = w1s.shape[2]
    row = pl.BlockSpec((tm, d), lambda i: (i, 0))
    return pl.pallas_call(
        functools.partial(_lora_kernel, shift=shift),
        grid=(m // tm,),
        in_specs=_shift_specs(tm, d, shift["n_prompt_tiles"]) + [
            pl.BlockSpec((3, d), lambda i: (0, 0)), pl.BlockSpec((3, d, r), lambda i: (0, 0, 0)),
            pl.BlockSpec((3, r, d), lambda i: (0, 0, 0)), pl.BlockSpec((3, d), lambda i: (0, 0))],
        out_specs=[row, row, row],
        out_shape=[jax.ShapeDtypeStruct((m, d), F32), jax.ShapeDtypeStruct((m, d), F32),
                   jax.ShapeDtypeStruct((m, d), BF16)],
        compiler_params=_params("parallel"),
        name="rwkv_lora",
    )(xn, xn, sx, mu3, w1s, w2s, bias3)


def _rwkv_kernel(r_ref, k_ref, v_ref, lw_ref, a_ref, g_ref, kk_ref, ka_ref, rk_ref, lnw_ref, lnb_ref,
                 e_ref, et_ref, s0_ref, o_ref, s_ref, sbd_ref, y_ref, *, L, nc):
    N = RWKV_HEAD_DIM
    c = pl.program_id(1)
    lane = lax.broadcasted_iota(jnp.int32, (1, LANES), 1)
    head0 = lane < N

    @pl.when(c == 0)
    def _():
        z = jnp.zeros((N, N), F32)
        for p in range(RWKV_PAIRS):
            top = jnp.concatenate([s0_ref[0, 2 * p], z], axis=1)
            bot = jnp.concatenate([z, s0_ref[0, 2 * p + 1]], axis=1)
            sbd_ref[p] = jnp.concatenate([top, bot], axis=0)

    E = e_ref[...]
    ET = et_ref[...]

    def seg_sum_bcast(x):
        return _dot_sel(_dot_sel(x, E), ET)

    r = r_ref[0].astype(F32)
    k = k_ref[0].astype(F32)
    v = v_ref[0].astype(F32)
    lw = lw_ref[0]
    a = a_ref[0]
    kk = k * kk_ref[...]
    kk = kk / jnp.maximum(jnp.sqrt(seg_sum_bcast(kk * kk)), L2_EPS)
    k2 = k * (1.0 + (a - 1.0) * ka_ref[...])

    row = lax.broadcasted_iota(jnp.int32, (L, L), 0)
    col = lax.broadcasted_iota(jnp.int32, (L, L), 1)
    tri = (col <= row).astype(BF16)
    logp = _sel_dot(tri, lw)
    p_incl = jnp.exp(logp)
    inv_p = jnp.exp(-logp)
    at = -kk * jnp.exp(logp - lw)
    bt = kk * a * inv_p
    kt = k2 * inv_p
    rt = r * p_incl
    p_last = p_incl[L - 1:L, :]

    rows2 = 2 * L
    row2 = lax.broadcasted_iota(jnp.int32, (rows2, rows2), 0)
    col2 = lax.broadcasted_iota(jnp.int32, (rows2, rows2), 1)
    same = (row2 >= L) == (col2 >= L)
    strict = same & (col2 < row2)
    incl = same & (col2 <= row2)
    eye = (row2 == col2).astype(F32)
    n_sq = max(int(math.ceil(math.log2(L))) - 1, 0)

    def stack(x):
        return jnp.concatenate([jnp.where(head0, x, 0.0), jnp.where(head0, 0.0, x)], axis=0)

    pairs = range(RWKV_PAIRS)
    sls = [slice(p * LANES, (p + 1) * LANES) for p in pairs]
    xar = [jnp.concatenate([stack(at[:, s]), stack(rt[:, s])], axis=0).astype(BF16) for s in sls]
    ybk = [jnp.concatenate([stack(bt[:, s]), stack(kt[:, s])], axis=0).astype(BF16) for s in sls]
    vr = [stack(v[:, s]).astype(BF16) for s in sls]
    sbd = [sbd_ref[p] for p in pairs]
    sb = [x.astype(BF16) for x in sbd]

    if rows2 % LANES == 0:
        big = [_dot_nt(xar[p], ybk[p]) for p in pairs]
        blk = lambda p, i, j: big[p][i * rows2:(i + 1) * rows2, j * rows2:(j + 1) * rows2]
    else:
        blk = lambda p, i, j: _dot_nt(xar[p][i * rows2:(i + 1) * rows2], ybk[p][j * rows2:(j + 1) * rows2])
    n_ab = [jnp.where(strict, blk(p, 0, 0), 0.0) for p in pairs]
    a_akrk = [jnp.concatenate([jnp.where(strict, blk(p, 0, 1), 0.0), jnp.where(incl, blk(p, 1, 1), 0.0)],
                              axis=0).astype(BF16) for p in pairs]
    a_rb = [jnp.where(incl, blk(p, 1, 0), 0.0).astype(BF16) for p in pairs]

    t_inv = [eye + n for n in n_ab]
    pw = [n.astype(BF16) for n in n_ab]
    for _ in range(n_sq):
        pw = [_dot(x, x).astype(BF16) for x in pw]
        t_inv = [t + _dot(t.astype(BF16), x) for t, x in zip(t_inv, pw)]

    xs = [_dot_nt(xar[p], sb[p]) for p in pairs]
    av = [_dot(a_akrk[p], vr[p]) for p in pairs]
    ub = [_dot(t_inv[p].astype(BF16), (xs[p][:rows2] + av[p][:rows2]).astype(BF16)).astype(BF16) for p in pairs]
    yr = [xs[p][rows2:] + _dot(a_rb[p], ub[p]) + av[p][rows2:] for p in pairs]
    for p in pairs:
        y_ref[:, sls[p]] = yr[p][:L] + yr[p][L:]
    for p in pairs:
        upd = _dot_tn(jnp.concatenate([ub[p], vr[p]], axis=0), ybk[p])
        sbd_ref[p] = (sbd[p] + upd) * p_last[:, sls[p]]

    y = y_ref[...]
    mean = seg_sum_bcast(y) * (1.0 / N)
    yc = y - mean
    var = seg_sum_bcast(yc * yc) * (1.0 / N)
    yn = yc * lax.rsqrt(var + RWKV_GN_EPS) * lnw_ref[...] + lnb_ref[...]
    bonus = seg_sum_bcast(r * k2 * rk_ref[...]) * v
    o_ref[0] = ((yn + bonus) * g_ref[0].astype(F32)).astype(o_ref.dtype)

    @pl.when(c == nc - 1)
    def _():
        for p in range(RWKV_PAIRS):
            blk_p = sbd_ref[p]
            s_ref[0, 2 * p] = blk_p[:N, :N]
            s_ref[0, 2 * p + 1] = blk_p[N:, N:]


def _rwkv(rkv3, lw3, a3, g3, k_k, k_a, r_k, ln_w, ln_b, s0, *, n_seq, nc, L, row0):
    D, H, N = D_MODEL, RWKV_HEADS, RWKV_HEAD_DIM
    onehot = (jnp.arange(D)[:, None] // N == jnp.arange(H)[None, :]).astype(BF16)

    def rows(colblk):
        return lambda s, c: (row0 + s * nc + c, 0, colblk)

    vec = pl.BlockSpec((1, D), lambda s, c: (0, 0))
    state = lambda s, c: (s, 0, 0, 0)
    in_specs = [
        pl.BlockSpec((1, L, D), rows(0)), pl.BlockSpec((1, L, D), rows(1)), pl.BlockSpec((1, L, D), rows(2)),
        pl.BlockSpec((1, L, D), rows(0)), pl.BlockSpec((1, L, D), rows(0)), pl.BlockSpec((1, L, D), rows(0)),
        vec, vec, vec, vec, vec,
        pl.BlockSpec((D, H), lambda s, c: (0, 0)), pl.BlockSpec((H, D), lambda s, c: (0, 0)),
        pl.BlockSpec((1, H, N, N), state),
    ]
    out_specs = [
        pl.BlockSpec((1, L, D), lambda s, c: (s * nc + c, 0, 0)),
        pl.BlockSpec((1, H, N, N), state),
    ]
    out_shape = [
        jax.ShapeDtypeStruct((n_seq * nc, L, D), BF16),
        jax.ShapeDtypeStruct((n_seq, H, N, N), F32),
    ]
    return pl.pallas_call(
        functools.partial(_rwkv_kernel, L=L, nc=nc),
        grid=(n_seq, nc),
        in_specs=in_specs,
        out_specs=out_specs,
        out_shape=out_shape,
        scratch_shapes=[pltpu.VMEM((RWKV_PAIRS, LANES, LANES), F32), pltpu.VMEM((L, D), F32)],
        compiler_params=_params("parallel", "arbitrary"),
        name="rwkv7_chunks",
    )(rkv3, rkv3, rkv3, lw3, a3, g3, k_k.reshape(1, D), k_a.reshape(1, D), r_k.reshape(1, D),
      ln_w.reshape(1, D), ln_b.reshape(1, D), onehot, onehot.T, s0)


def _mlstm_gate_epilogue(acc, bias):
    z = GATE_SOFTCAP * jnp.tanh((acc + bias) / GATE_SOFTCAP)
    lane = lax.broadcasted_iota(jnp.int32, z.shape, 1)
    return jnp.where(lane < MLSTM_HEADS, z, jax.nn.log_sigmoid(z))


def _ffn_ple(h, hn, p_pair, w_gate, w_up, w_down, g_ple, w_ple_gate, w_ple_proj, next_gain, *, split_out):
    d_ff = w_down.shape[0]
    act = _proj(hn, [w_gate.astype(BF16), w_up.astype(BF16)], out_dtype=BF16, tm=1024, tn=512,
                epilogue=lambda g, u: jax.nn.silu(g) * u, name="swiglu_up")
    h, hn = _res(act, w_down.astype(BF16), h, g_ple, tm=512, tk=d_ff // 4, hn_dtype=BF16, name="swiglu_down")
    return _res(hn, w_ple_gate.astype(BF16), h, next_gain, tm=512, tk=w_ple_gate.shape[0], hn_dtype=F32,
                x2=p_pair, w2=w_ple_proj.astype(BF16), emit_h=split_out is None, split_out=split_out, name="ple")


def kernel(x_prompt, x_sample, state_mlstm_C, state_mlstm_n, state_mlstm_m, state_rwkv_S, state_rwkv_shift,
           p_prompt, p_sample, norm_mix, norm_ffn, norm_ple, norm_final, ffn_w_gate, ffn_w_up, ffn_w_down,
           ple_w_proj, ple_w_gate, mlstm_w_q, mlstm_w_k, mlstm_w_v, mlstm_w_igate, mlstm_b_igate,
           mlstm_w_fgate, mlstm_b_fgate, mlstm_w_ogate, mlstm_norm_w, mlstm_w_out, rwkv_mu, rwkv_w_r,
           rwkv_w_k, rwkv_w_v, rwkv_w_o, rwkv_w0, rwkv_w1, rwkv_w2, rwkv_a0, rwkv_a1, rwkv_a2, rwkv_g1,
           rwkv_g2, rwkv_k_k, rwkv_k_a, rwkv_r_k, rwkv_ln_w, rwkv_ln_b):
    D = D_MODEL
    B, T, _ = x_prompt.shape
    BS, TS, _ = x_sample.shape
    MP, MS = B * T, BS * TS
    M = MP + MS
    H, DK, DV = MLSTM_HEADS, MLSTM_DQK, MLSTM_DV
    RH, RN = RWKV_HEADS, RWKV_HEAD_DIM
    PD = p_prompt.shape[-1]

    x_pair = (x_prompt.reshape(MP, D), x_sample.reshape(MS, D))
    p_pairs = [(p_prompt[i].reshape(MP, PD), p_sample[i].reshape(MS, PD)) for i in range(2)]

    hn = _norm(x_pair, norm_mix[0], BF16)
    w_qkvo = jnp.concatenate([mlstm_w_q[0], mlstm_w_k[0], mlstm_w_v[0], mlstm_w_ogate[0]], axis=1).astype(BF16)
    col_scale = jnp.concatenate([jnp.ones((H * DK,), F32), jnp.full((H * DK,), DK ** -0.5, F32),
                                 jnp.ones((2 * H * DV,), F32)])
    qkvo = _proj(hn, [w_qkvo], out_dtype=BF16, tm=1024, tn=1024, extras=[col_scale],
                 epilogue=lambda a, s: a * s, name="mlstm_qkvo")
    w_gates = jnp.zeros((D, LANES), F32).at[:, :H].set(mlstm_w_igate[0]).at[:, H:2 * H].set(mlstm_w_fgate[0])
    b_gates = jnp.zeros((LANES,), F32).at[:H].set(mlstm_b_igate[0]).at[H:2 * H].set(mlstm_b_fgate[0])
    gates = _proj(hn, [w_gates.astype(BF16)], out_dtype=F32, tm=1024, tn=LANES, extras=[b_gates],
                  epilogue=_mlstm_gate_epilogue, name="mlstm_gates")

    LP = math.gcd(T, MLSTM_CHUNK)
    LS = math.gcd(TS, MLSTM_CHUNK)
    gp = gates.reshape(M // LP, LP, LANES)
    gs = gates.reshape(M // LS, LS, LANES)
    hm_p, C_p, n_p, m_p = _mlstm(qkvo.reshape(M // LP, LP, -1), gp, gp.transpose(0, 2, 1), mlstm_norm_w[0],
                                 jnp.zeros((B, H, DK, DV), F32), jnp.zeros((B, H, DK), F32),
                                 jnp.zeros((B, H), F32), n_seq=B, nc=T // LP, L=LP, row0=0)
    hm_s, C_s, n_s, m_s = _mlstm(qkvo.reshape(M // LS, LS, -1), gs, gs.transpose(0, 2, 1), mlstm_norm_w[0],
                                 state_mlstm_C[0], state_mlstm_n[0], state_mlstm_m[0],
                                 n_seq=BS, nc=TS // LS, L=LS, row0=MP // LS)
    h, hn = _res((hm_p.reshape(MP, D), hm_s.reshape(MS, D)), mlstm_w_out[0].astype(BF16), x_pair, norm_ffn[0],
                 tm=512, tk=D, hn_dtype=BF16, name="mlstm_out")
    h, xn = _ffn_ple(h, hn, p_pairs[0], ffn_w_gate[0], ffn_w_up[0], ffn_w_down[0], norm_ple[0], ple_w_gate[0],
                     ple_w_proj[0], norm_mix[1], split_out=None)

    shift_p = xn[T - 1:MP:T]
    shift_s = xn[MP + TS - 1::TS]
    tm_r = _row_tile(math.gcd(MP, MS), 512)
    shift = dict(tm=tm_r, t_prompt=T, t_sample=TS, n_prompt_tiles=MP // tm_r)
    sx = jnp.repeat(state_rwkv_shift[0], TS, axis=0)
    mu = rwkv_mu[0]
    w_rkv = jnp.concatenate([rwkv_w_r[0], rwkv_w_k[0], rwkv_w_v[0]], axis=1).astype(BF16)
    rkv = _rkv(xn, sx, jnp.stack([mu[0], mu[2], mu[3]]), w_rkv, tm=tm_r, tn=1024, shift=shift)

    rank =max(-(-w.shape[1] // LANES) * LANES for w in (rwkv_w1[0], rwkv_a1[0], rwkv_g1[0]))

    def pad_to(w1, w2):
        r = w1.shape[1]
        return (jnp.pad(w1, ((0, 0), (0, rank - r))).astype(BF16), jnp.pad(w2, ((0, rank - r), (0, 0))).astype(BF16))

    lora_w = [pad_to(rwkv_w1[0], rwkv_w2[0]), pad_to(rwkv_a1[0], rwkv_a2[0]), pad_to(rwkv_g1[0], rwkv_g2[0])]
    lw, aa, gg = _lora(xn, sx, jnp.stack([mu[1], mu[4], mu[5]]), jnp.stack([w[0] for w in lora_w]),
                       jnp.stack([w[1] for w in lora_w]),
                       jnp.stack([rwkv_w0[0], rwkv_a0[0], jnp.zeros((D,), F32)]), tm=tm_r, shift=shift)

    LRP = math.gcd(T, RWKV_CHUNK)
    LRS = math.gcd(TS, RWKV_CHUNK)

    def run_rwkv(L, n_seq, t_len, row0, s0):
        return _rwkv(rkv.reshape(M // L, L, 3 * D), lw.reshape(M // L, L, D), aa.reshape(M // L, L, D),
                     gg.reshape(M // L, L, D), rwkv_k_k[0], rwkv_k_a[0], rwkv_r_k[0], rwkv_ln_w[0], rwkv_ln_b[0],
                     s0, n_seq=n_seq, nc=t_len // L, L=L, row0=row0)

    y_p, S_p = run_rwkv(LRP, B, T, 0, jnp.zeros((B, RH, RN, RN), F32))
    y_s, S_s = run_rwkv(LRS, BS, TS, MP // LRS, state_rwkv_S[0])
    h, hn = _res((y_p.reshape(MP, D), y_s.reshape(MS, D)), rwkv_w_o[0].astype(BF16), h, norm_ffn[1],
                 tm=512, tk=D, hn_dtype=BF16, name="rwkv_out")
    y_prompt, y_sample = _ffn_ple(h, hn, p_pairs[1], ffn_w_gate[1], ffn_w_up[1], ffn_w_down[1], norm_ple[1],
                                  ple_w_gate[1], ple_w_proj[1], norm_final, split_out=MP)

    return (y_prompt.reshape(B, T, D), y_sample.reshape(BS, TS, D),
            C_p[None], n_p[None], m_p.reshape(1, B, H), S_p[None], shift_p[None],
            C_s[None], n_s[None], m_s.reshape(1, BS, H), S_s[None], shift_s[None])
```

```python
import functools
import math

import jax
import jax.numpy as jnp
from jax import lax
from jax.experimental import pallas as pl
from jax.experimental.pallas import tpu as pltpu

F32 = jnp.float32
BF16 = jnp.bfloat16

D_MODEL = 2048
MLSTM_HEADS = 8
MLSTM_DQK = 128
MLSTM_DV = 256
MLSTM_CHUNK = 256
GATE_SOFTCAP = 15.0
RWKV_HEAD_DIM = 64
RWKV_HEADS = 32
RWKV_PAIRS = RWKV_HEADS // 2
RWKV_CHUNK = 64
NORM_EPS = 1e-6
RWKV_GN_EPS = 64e-5
L2_EPS = 1e-12
LANES = 128
SUBLANES = 8
VMEM_LIMIT_BYTES = 56 * 1024 * 1024


def _params(*sem):
    return pltpu.CompilerParams(dimension_semantics=sem, vmem_limit_bytes=VMEM_LIMIT_BYTES)


def _row_tile(m, preferred):
    t = preferred
    while m % t:
        t -= LANES
    return t


def _rms(x, g):
    return x * lax.rsqrt(jnp.mean(x * x, axis=-1, keepdims=True) + NORM_EPS) * g


def _dot(a, b):
    return jnp.dot(a, b, preferred_element_type=F32)


def _dot_nt(a, b):
    return lax.dot_general(a, b, (((1,), (1,)), ((), ())), preferred_element_type=F32)


def _dot_tn(a, b):
    return lax.dot_general(a, b, (((0,), (0,)), ((), ())), preferred_element_type=F32)


def _split3(x):
    hi = x.astype(BF16)
    r1 = x - hi.astype(F32)
    mid = r1.astype(BF16)
    lo = (r1 - mid.astype(F32)).astype(BF16)
    return hi, mid, lo


def _dot_sel(x, sel):
    hi, mid, lo = _split3(x)
    return _dot(hi, sel) + _dot(mid, sel) + _dot(lo, sel)


def _sel_dot(sel, x):
    hi, mid, lo = _split3(x)
    return _dot(sel, hi) + _dot(sel, mid) + _dot(sel, lo)


def _sel_dot_nt(x, sel):
    hi, mid, lo = _split3(x)
    return _dot_nt(hi, sel) + _dot_nt(mid, sel) + _dot_nt(lo, sel)


def _row_specs(x, tm, cols, col_index):
    if isinstance(x, tuple):
        a, b = x
        na = a.shape[0] // tm
        return ([pl.BlockSpec((tm, cols), lambda i, *r: (jnp.minimum(i, na - 1), col_index(*r))),
                 pl.BlockSpec((tm, cols), lambda i, *r: (jnp.maximum(i - na, 0), col_index(*r)))], [a, b], na)
    return [pl.BlockSpec((tm, cols), lambda i, *r: (i, col_index(*r)))], [x], None


def _rows(x):
    return sum(a.shape[0] for a in x) if isinstance(x, tuple) else x.shape[0]


def _pair_row_tile(operands, preferred, extra_rows=()):
    halves = [a.shape[0] for x in operands if x is not None for a in (x if isinstance(x, tuple) else (x,))]
    return _row_tile(math.gcd(*halves, *extra_rows), preferred)


def _load_rows(refs, na):
    if na is None:
        return refs[0][...]
    return jnp.where(pl.program_id(0) < na, refs[0][...], refs[1][...])


def _store_rows(refs, na, val):
    if na is None:
        refs[0][...] = val.astype(refs[0].dtype)
        return

    @pl.when(pl.program_id(0) < na)
    def _():
        refs[0][...] = val.astype(refs[0].dtype)

    @pl.when(pl.program_id(0) >= na)
    def _():
        refs[1][...] = val.astype(refs[1].dtype)


_zero_col = lambda *r: 0


def _norm_kernel(*refs, na):
    nx = 1 if na is None else 2
    g_ref, o_ref = refs[nx], refs[nx + 1]
    o_ref[...] = _rms(_load_rows(refs[:nx], na), g_ref[...]).astype(o_ref.dtype)


def _norm(x, g, out_dtype, tm=512):
    m = _rows(x)
    d = g.shape[0]
    tm = _pair_row_tile([x], tm)
    x_specs, x_args, na = _row_specs(x, tm, d, _zero_col)
    return pl.pallas_call(
        functools.partial(_norm_kernel, na=na),
        grid=(m // tm,),
        in_specs=x_specs + [pl.BlockSpec((1, d), lambda i: (0, 0))],
        out_specs=pl.BlockSpec((tm, d), lambda i: (i, 0)),
        out_shape=jax.ShapeDtypeStruct((m, d), out_dtype),
        compiler_params=_params("parallel"),
        name="rmsnorm",
    )(*x_args, g.reshape(1, d))


def _proj_kernel(*refs, nw, ne, epilogue):
    x_ref = refs[0]
    ws = refs[1:1 + nw]
    es = refs[1 + nw:1 + nw + ne]
    o_ref = refs[1 + nw + ne]
    xb = x_ref[...]
    accs = [_dot(xb, w[...].astype(BF16)) for w in ws]
    o_ref[...] = epilogue(*accs, *[e[...] for e in es]).astype(o_ref.dtype)


def _proj(x, ws, *, out_dtype, tm, tn, extras=(), epilogue=lambda a: a, name="proj"):
    m, k = x.shape
    n = ws[0].shape[1]
    tm = _row_tile(m, tm)
    in_specs = ([pl.BlockSpec((tm, k), lambda i, j: (i, 0))]
                + [pl.BlockSpec((k, tn), lambda i, j: (0, j)) for _ in ws]
                + [pl.BlockSpec((1, tn), lambda i, j: (0, j)) for _ in extras])
    return pl.pallas_call(
        functools.partial(_proj_kernel, nw=len(ws), ne=len(extras), epilogue=epilogue),
        grid=(m // tm, n // tn),
        in_specs=in_specs,
        out_specs=pl.BlockSpec((tm, tn), lambda i, j: (i, j)),
        out_shape=jax.ShapeDtypeStruct((m, n), out_dtype),
        compiler_params=_params("parallel", "arbitrary"),
        name=name,
    )(x, *ws, *[e.reshape(1, n) for e in extras])


def _res_kernel(*refs, nk, gated, emit_h, na_x, na_res, na_x2, na_out):
    pos = [0]

    def take(n):
        out = refs[pos[0]:pos[0] + n]
        pos[0] += n
        return out

    width = lambda na: 1 if na is None else 2
    x_refs = take(width(na_x))
    (w_ref,) = take(1)
    res_refs = take(width(na_res))
    (g_ref,) = take(1)
    if gated:
        x2_refs = take(width(na_x2))
        (w2_ref,) = take(1)
    h_refs = take(1) if emit_h else ()
    hn_refs = take(width(na_out))
    acc_refs = take(1) if nk > 1 else ()
    k = pl.program_id(1)

    part = _dot(_load_rows(x_refs, na_x), w_ref[...])
    if nk > 1:
        acc_ref = acc_refs[0]

        @pl.when(k == 0)
        def _():
            acc_ref[...] = part

        @pl.when(k > 0)
        def _():
            acc_ref[...] += part

    @pl.when(k == nk - 1)
    def _():
        a = acc_refs[0][...] if nk > 1 else part
        res = _load_rows(res_refs, na_res)
        if gated:
            h = res + jax.nn.sigmoid(a) * _dot(_load_rows(x2_refs, na_x2).astype(BF16), w2_ref[...])
        else:
            h = res + a
        if emit_h:
            h_refs[0][...] = h
        _store_rows(hn_refs, na_out, _rms(h, g_ref[...]))


def _res(x, w, res, gain, *, tm, tk, hn_dtype, x2=None, w2=None, emit_h=True, split_out=None, name="res"):
    m = _rows(x)
    kdim, d = w.shape
    tm = _pair_row_tile([x, res, x2], tm, () if split_out is None else (split_out, m - split_out))
    nk = kdim // tk
    gated = x2 is not None
    x_specs, x_args, na_x = _row_specs(x, tm, tk, lambda k: k)
    res_specs, res_args, na_res = _row_specs(res, tm, d, _zero_col)
    const = lambda i, k: (0, 0)
    w_mode = dict(pipeline_mode=pl.Buffered(1)) if nk == 1 else {}
    in_specs = (x_specs + [pl.BlockSpec((tk, d), lambda i, k: (k, 0), **w_mode)]
                + res_specs + [pl.BlockSpec((1, d), const)])
    args = x_args + [w] + res_args + [gain.reshape(1, d)]
    na_x2 = None
    if gated:
        k2 = w2.shape[0]
        x2_specs, x2_args, na_x2 = _row_specs(x2, tm, k2, _zero_col)
        in_specs += x2_specs + [pl.BlockSpec((k2, d), const, pipeline_mode=pl.Buffered(1))]
        args += x2_args + [w2]
    out_specs, out_shape = [], []
    if emit_h:
        out_specs.append(pl.BlockSpec((tm, d), lambda i, k: (i, 0)))
        out_shape.append(jax.ShapeDtypeStruct((m, d), F32))
    na_out = None
    if split_out is None:
        out_specs.append(pl.BlockSpec((tm, d), lambda i, k: (i, 0)))
        out_shape.append(jax.ShapeDtypeStruct((m, d), hn_dtype))
    else:
        na_out = split_out // tm
        out_specs += [pl.BlockSpec((tm, d), lambda i, k: (jnp.minimum(i, na_out - 1), 0)),
                      pl.BlockSpec((tm, d), lambda i, k: (jnp.maximum(i - na_out, 0), 0))]
        out_shape += [jax.ShapeDtypeStruct((split_out, d), hn_dtype),
                      jax.ShapeDtypeStruct((m - split_out, d), hn_dtype)]
    return pl.pallas_call(
        functools.partial(_res_kernel, nk=nk, gated=gated, emit_h=emit_h, na_x=na_x, na_res=na_res,
                          na_x2=na_x2, na_out=na_out),
        grid=(m // tm, nk),
        in_specs=in_specs,
        out_specs=out_specs,
        out_shape=out_shape,
        scratch_shapes=[pltpu.VMEM((tm, d), F32)] if nk > 1 else [],
        compiler_params=_params("parallel", "arbitrary"),
        name=name,
    )(*args)


def _mlstm_kernel(q_ref, k_ref, v_ref, og_ref, g_ref, gt_ref, nw_ref, c0_ref, n0_ref, m0_ref,
                  h_ref, c_ref, n_ref, m_ref, *, L):
    H, DK, DV = MLSTM_HEADS, MLSTM_DQK, MLSTM_DV

    @pl.when(pl.program_id(1) == 0)
    def _():
        c_ref[...] = c0_ref[...]
        n_ref[...] = n0_ref[...]
        m_ref[...] = m0_ref[...]

    q = q_ref[0]
    k = k_ref[0]
    v = v_ref[0]
    og = og_ref[0]
    G = g_ref[0]
    GT = gt_ref[0]
    nw = nw_ref[...]
    m_prev = m_ref[0]

    row = lax.broadcasted_iota(jnp.int32, (L, L), 0)
    col = lax.broadcasted_iota(jnp.int32, (L, L), 1)
    causal = col <= row
    tri = causal.astype(BF16)
    b_col = _sel_dot(tri, G)
    b_row = _sel_dot_nt(GT, tri)
    lane_h = lax.broadcasted_iota(jnp.int32, (1, H), 1)

    hs = range(H)
    qh = [q[:, h * DK:(h + 1) * DK] for h in hs]
    kh = [k[:, h * DK:(h + 1) * DK] for h in hs]
    vh = [v[:, h * DV:(h + 1) * DV] for h in hs]
    ch = [c_ref[0, h] for h in hs]
    nh = [n_ref[0, h:h + 1, :] for h in hs]
    bc = [b_col[:, H + h:H + h + 1] for h in hs]
    li_c = [G[:, h:h + 1] for h in hs]
    m_h = [m_prev[:, h:h + 1] for h in hs]
    qk = [_dot_nt(qh[h], kh[h]) for h in hs]
    qc = [_dot(qh[h], ch[h].astype(BF16)) for h in hs]
    dlog = [jnp.where(causal, bc[h] - b_row[H + h:H + h + 1, :] + GT[h:h + 1, :], -jnp.inf) for h in hs]
    a = [bc[h] + m_h[h] for h in hs]
    m_t = [jnp.maximum(a[h], jnp.max(dlog[h], axis=1, keepdims=True)) for h in hs]
    w_inter = [jnp.exp(a[h] - m_t[h]) for h in hs]
    s = [qk[h] * jnp.exp(dlog[h] - m_t[h]) for h in hs]
    sv = [_dot(s[h].astype(BF16), vh[h]) for h in hs]
    m_new = [m_t[h][L - 1:L, :] for h in hs]
    b_last = [bc[h][L - 1:L, :] for h in hs]
    wk = [jnp.exp(b_last[h] - bc[h] + li_c[h] - m_new[h]) for h in hs]
    kv = [_dot_tn(kh[h], (vh[h].astype(F32) * wk[h]).astype(BF16)) for h in hs]
    decay = [jnp.exp(b_last[h] + m_h[h] - m_new[h]) for h in hs]
    for h in hs:
        c_ref[0, h] = decay[h] * ch[h] + kv[h]
        n_ref[0, h:h + 1, :] = decay[h] * nh[h] + jnp.sum(kh[h].astype(F32) * wk[h], axis=0, keepdims=True)
    m_out = jnp.zeros((1, H), F32)
    for h in hs:
        m_out = jnp.where(lane_h == h, m_new[h], m_out)
    m_ref[0] = m_out

    for h in hs:
        num = w_inter[h] * qc[h] + sv[h]
        den = (w_inter[h] * jnp.sum(qh[h].astype(F32) * nh[h], axis=1, keepdims=True)
               + jnp.sum(s[h], axis=1, keepdims=True))
        hh = num / jnp.maximum(jnp.abs(den), jnp.exp(-m_t[h]))
        hh = hh * lax.rsqrt(jnp.mean(hh * hh, axis=1, keepdims=True) + NORM_EPS)
        o = jax.nn.sigmoid(og[:, h * DV:(h + 1) * DV].astype(F32))
        h_ref[0, :, h * DV:(h + 1) * DV] = (o * (hh * nw[:, h * DV:(h + 1) * DV])).astype(h_ref.dtype)


def _mlstm(qkvo3, g3, gt3, nw, c0, n0, m0, *, n_seq, nc, L, row0):
    H, DK, DV = MLSTM_HEADS, MLSTM_DQK, MLSTM_DV
    nqk, nv = H * DK, H * DV

    def rows(colblk):
        return lambda s, c: (row0 + s * nc + c, 0, colblk)

    state = lambda s, c: (s, 0, 0, 0)
    state3 = lambda s, c: (s, 0, 0)
    in_specs = [
        pl.BlockSpec((1, L, nqk), rows(0)),
        pl.BlockSpec((1, L, nqk), rows(1)),
        pl.BlockSpec((1, L, nv), rows(1)),
        pl.BlockSpec((1, L, nv), rows(2)),
        pl.BlockSpec((1, L, LANES), rows(0)),
        pl.BlockSpec((1, LANES, L), rows(0)),
        pl.BlockSpec((1, nv), lambda s, c: (0, 0)),
        pl.BlockSpec((1, H, DK, DV), state),
        pl.BlockSpec((1, H, DK), state3),
        pl.BlockSpec((1, 1, H), state3),
    ]
    out_specs = [
        pl.BlockSpec((1, L, nv), lambda s, c: (s * nc + c, 0, 0)),
        pl.BlockSpec((1, H, DK, DV), state),
        pl.BlockSpec((1, H, DK), state3),
        pl.BlockSpec((1, 1, H), state3),
    ]
    out_shape = [
        jax.ShapeDtypeStruct((n_seq * nc, L, nv), BF16),
        jax.ShapeDtypeStruct((n_seq, H, DK, DV), F32),
        jax.ShapeDtypeStruct((n_seq, H, DK), F32),
        jax.ShapeDtypeStruct((n_seq, 1, H), F32),
    ]
    return pl.pallas_call(
        functools.partial(_mlstm_kernel, L=L),
        grid=(n_seq, nc),
        in_specs=in_specs,
        out_specs=out_specs,
        out_shape=out_shape,
        compiler_params=_params("parallel", "arbitrary"),
        name="mlstm_chunks",
    )(qkvo3, qkvo3, qkvo3, qkvo3, g3, gt3, nw.reshape(1, nv), c0, n0, m0.reshape(n_seq, 1, H))


def _shifted(x_ref, prev_ref, sx_ref, *, tm, t_prompt, t_sample, n_prompt_tiles):
    i = pl.program_id(0)
    x = x_ref[...]
    rolled = pltpu.roll(x, 1, axis=0)
    rowi = lax.broadcasted_iota(jnp.int32, (tm, 1), 0)
    first = jnp.where((i * tm) % t_prompt == 0, 0.0, prev_ref[SUBLANES - 1:SUBLANES, :])
    xprev = jnp.where(rowi == 0, first, rolled)
    at_sample_start = jnp.logical_and(i >= n_prompt_tiles, rowi % t_sample == 0)
    return x, jnp.where(at_sample_start, sx_ref[...], xprev)


def _shift_specs(tm, d, n_prompt_tiles):
    return [pl.BlockSpec((tm, d), lambda i, *r: (i, 0)),
            pl.BlockSpec((SUBLANES, d), lambda i, *r: (jnp.maximum(i * (tm // SUBLANES) - 1, 0), 0)),
            pl.BlockSpec((tm, d), lambda i, *r: (jnp.maximum(i - n_prompt_tiles, 0), 0))]


def _rkv_kernel(x_ref, prev_ref, sx_ref, mu_ref, w_ref, o_ref, xb_ref, *, tiles_per_group, shift):
    j = pl.program_id(1)

    @pl.when(j == 0)
    def _():
        x, xp = _shifted(x_ref, prev_ref, sx_ref, **shift)
        dx = xp - x
        for c in range(3):
            xb_ref[c] = (x + dx * mu_ref[c:c + 1, :]).astype(BF16)

    o_ref[...] = _dot(xb_ref[j // tiles_per_group], w_ref[...]).astype(o_ref.dtype)


def _rkv(xn, sx, mu3, w_rkv, *, tm, tn, shift):
    m, d = xn.shape
    n = w_rkv.shape[1]
    return pl.pallas_call(
        functools.partial(_rkv_kernel, tiles_per_group=d // tn, shift=shift),
        grid=(m // tm, n // tn),
        in_specs=_shift_specs(tm, d, shift["n_prompt_tiles"]) + [
            pl.BlockSpec((3, d), lambda i, j: (0, 0)), pl.BlockSpec((d, tn), lambda i, j: (0, j))],
        out_specs=pl.BlockSpec((tm, tn), lambda i, j: (i, j)),
        out_shape=jax.ShapeDtypeStruct((m, n), BF16),
        scratch_shapes=[pltpu.VMEM((3, tm, d), BF16)],
        compiler_params=_params("parallel", "arbitrary"),
        name="rwkv_rkv",
    )(xn, xn, sx, mu3, w_rkv)


def _lora_kernel(x_ref, prev_ref, sx_ref, mu_ref, w1_ref, w2_ref, b_ref, lw_ref, a_ref, g_ref, *, shift):
    x, xp = _shifted(x_ref, prev_ref, sx_ref, **shift)
    dx = xp - x

    def branch(c, mid):
        xm = (x + dx * mu_ref[c:c + 1, :]).astype(BF16)
        hid = mid(_dot(xm, w1_ref[c])).astype(BF16)
        return b_ref[c:c + 1, :] + _dot(hid, w2_ref[c])

    lw_ref[...] = -jnp.exp(-jax.nn.softplus(-branch(0, jnp.tanh)) - 0.5)
    a_ref[...] = jax.nn.sigmoid(branch(1, lambda z: z))
    g_ref[...] = branch(2, jax.nn.sigmoid).astype(g_ref.dtype)


def _lora(xn, sx, mu3, w1s, w2s, bias3, *, tm, shift):
    m, d = xn.shape
    r = w1s.shape[2]
    row = pl.BlockSpec((tm, d), lambda i: (i, 0))
    return pl.pallas_call(
        functools.partial(_lora_kernel, shift=shift),
        grid=(m // tm,),
        in_specs=_shift_specs(tm, d, shift["n_prompt_tiles"]) + [
            pl.BlockSpec((3, d), lambda i: (0, 0)), pl.BlockSpec((3, d, r), lambda i: (0, 0, 0)),
            pl.BlockSpec((3, r, d), lambda i: (0, 0, 0)), pl.BlockSpec((3, d), lambda i: (0, 0))],
        out_specs=[row, row, row],
        out_shape=[jax.ShapeDtypeStruct((m, d), F32), jax.ShapeDtypeStruct((m, d), F32),
                   jax.ShapeDtypeStruct((m, d), BF16)],
        compiler_params=_params("parallel"),
        name="rwkv_lora",
    )(xn, xn, sx, mu3, w1s, w2s, bias3)


def _rwkv_kernel(r_ref, k_ref, v_ref, lw_ref, a_ref, g_ref, kk_ref, ka_ref, rk_ref, lnw_ref, lnb_ref,
                 e_ref, et_ref, s0_ref, o_ref, s_ref, sbd_ref, y_ref, *, L, nc):
    N = RWKV_HEAD_DIM
    c = pl.program_id(1)
    lane = lax.broadcasted_iota(jnp.int32, (1, LANES), 1)
    head0 = lane < N

    @pl.when(c == 0)
    def _():
        z = jnp.zeros((N, N), F32)
        for p in range(RWKV_PAIRS):
            top = jnp.concatenate([s0_ref[0, 2 * p], z], axis=1)
            bot = jnp.concatenate([z, s0_ref[0, 2 * p + 1]], axis=1)
            sbd_ref[p] = jnp.concatenate([top, bot], axis=0)

    E = e_ref[...]
    ET = et_ref[...]

    def seg_sum_bcast(x):
        return _dot_sel(_dot_sel(x, E), ET)

    r = r_ref[0].astype(F32)
    k = k_ref[0].astype(F32)
    v = v_ref[0].astype(F32)
    lw = lw_ref[0]
    a = a_ref[0]
    kk = k * kk_ref[...]
    kk = kk / jnp.maximum(jnp.sqrt(seg_sum_bcast(kk * kk)), L2_EPS)
    k2 = k * (1.0 + (a - 1.0) * ka_ref[...])

    row = lax.broadcasted_iota(jnp.int32, (L, L), 0)
    col = lax.broadcasted_iota(jnp.int32, (L, L), 1)
    tri = (col <= row).astype(BF16)
    logp = _sel_dot(tri, lw)
    p_incl = jnp.exp(logp)
    inv_p = jnp.exp(-logp)
    at = -kk * jnp.exp(logp - lw)
    bt = kk * a * inv_p
    kt = k2 * inv_p
    rt = r * p_incl
    p_last = p_incl[L - 1:L, :]

    rows2 = 2 * L
    row2 = lax.broadcasted_iota(jnp.int32, (rows2, rows2), 0)
    col2 = lax.broadcasted_iota(jnp.int32, (rows2, rows2), 1)
    same = (row2 >= L) == (col2 >= L)
    strict = same & (col2 < row2)
    incl = same & (col2 <= row2)
    eye = (row2 == col2).astype(F32)
    n_sq = max(int(math.ceil(math.log2(L))) - 1, 0)

    def stack(x):
        return jnp.concatenate([jnp.where(head0, x, 0.0), jnp.where(head0, 0.0, x)], axis=0)

    pairs = range(RWKV_PAIRS)
    sls = [slice(p * LANES, (p + 1) * LANES) for p in pairs]
    xar = [jnp.concatenate([stack(at[:, s]), stack(rt[:, s])], axis=0).astype(BF16) for s in sls]
    ybk = [jnp.concatenate([stack(bt[:, s]), stack(kt[:, s])], axis=0).astype(BF16) for s in sls]
    vr = [stack(v[:, s]).astype(BF16) for s in sls]
    sbd = [sbd_ref[p] for p in pairs]
    sb = [x.astype(BF16) for x in sbd]

    if rows2 % LANES == 0:
        big = [_dot_nt(xar[p], ybk[p]) for p in pairs]
        blk = lambda p, i, j: big[p][i * rows2:(i + 1) * rows2, j * rows2:(j + 1) * rows2]
    else:
        blk = lambda p, i, j: _dot_nt(xar[p][i * rows2:(i + 1) * rows2], ybk[p][j * rows2:(j + 1) * rows2])
    n_ab = [jnp.where(strict, blk(p, 0, 0), 0.0) for p in pairs]
    a_akrk = [jnp.concatenate([jnp.where(strict, blk(p, 0, 1), 0.0), jnp.where(incl, blk(p, 1, 1), 0.0)],
                              axis=0).astype(BF16) for p in pairs]
    a_rb = [jnp.where(incl, blk(p, 1, 0), 0.0).astype(BF16) for p in pairs]

    t_inv = [eye + n for n in n_ab]
    pw = [n.astype(BF16) for n in n_ab]
    for _ in range(n_sq):
        pw = [_dot(x, x).astype(BF16) for x in pw]
        t_inv = [t + _dot(t.astype(BF16), x) for t, x in zip(t_inv, pw)]

    xs = [_dot_nt(xar[p], sb[p]) for p in pairs]
    av = [_dot(a_akrk[p], vr[p]) for p in pairs]
    ub = [_dot(t_inv[p].astype(BF16), (xs[p][:rows2] + av[p][:rows2]).astype(BF16)).astype(BF16) for p in pairs]
    yr = [xs[p][rows2:] + _dot(a_rb[p], ub[p]) + av[p][rows2:] for p in pairs]
    for p in pairs:
        y_ref[:, sls[p]] = yr[p][:L] + yr[p][L:]
    for p in pairs:
        upd = _dot_tn(jnp.concatenate([ub[p], vr[p]], axis=0), ybk[p])
        sbd_ref[p] = (sbd[p] + upd) * p_last[:, sls[p]]

    y = y_ref[...]
    mean = seg_sum_bcast(y) * (1.0 / N)
    yc = y - mean
    var = seg_sum_bcast(yc * yc) * (1.0 / N)
    yn = yc * lax.rsqrt(var + RWKV_GN_EPS) * lnw_ref[...] + lnb_ref[...]
    bonus = seg_sum_bcast(r * k2 * rk_ref[...]) * v
    o_ref[0] = ((yn + bonus) * g_ref[0].astype(F32)).astype(o_ref.dtype)

    @pl.when(c == nc - 1)
    def _():
        for p in range(RWKV_PAIRS):
            blk_p = sbd_ref[p]
            s_ref[0, 2 * p] = blk_p[:N, :N]
            s_ref[0, 2 * p + 1] = blk_p[N:, N:]


def _rwkv(rkv3, lw3, a3, g3, k_k, k_a, r_k, ln_w, ln_b, s0, *, n_seq, nc, L, row0):
    D, H, N = D_MODEL, RWKV_HEADS, RWKV_HEAD_DIM
    onehot = (jnp.arange(D)[:, None] // N == jnp.arange(H)[None, :]).astype(BF16)

    def rows(colblk):
        return lambda s, c: (row0 + s * nc + c, 0, colblk)

    vec = pl.BlockSpec((1, D), lambda s, c: (0, 0))
    state = lambda s, c: (s, 0, 0, 0)
    in_specs = [
        pl.BlockSpec((1, L, D), rows(0)), pl.BlockSpec((1, L, D), rows(1)), pl.BlockSpec((1, L, D), rows(2)),
        pl.BlockSpec((1, L, D), rows(0)), pl.BlockSpec((1, L, D), rows(0)), pl.BlockSpec((1, L, D), rows(0)),
        vec, vec, vec, vec, vec,
        pl.BlockSpec((D, H), lambda s, c: (0, 0)), pl.BlockSpec((H, D), lambda s, c: (0, 0)),
        pl.BlockSpec((1, H, N, N), state),
    ]
    out_specs = [
        pl.BlockSpec((1, L, D), lambda s, c: (s * nc + c, 0, 0)),
        pl.BlockSpec((1, H, N, N), state),
    ]
    out_shape = [
        jax.ShapeDtypeStruct((n_seq * nc, L, D), BF16),
        jax.ShapeDtypeStruct((n_seq, H, N, N), F32),
    ]
    return pl.pallas_call(
        functools.partial(_rwkv_kernel, L=L, nc=nc),
        grid=(n_seq, nc),
        in_specs=in_specs,
        out_specs=out_specs,
        out_shape=out_shape,
        scratch_shapes=[pltpu.VMEM((RWKV_PAIRS, LANES, LANES), F32), pltpu.VMEM((L, D), F32)],
        compiler_params=_params("parallel", "arbitrary"),
        name="rwkv7_chunks",
    )(rkv3, rkv3, rkv3, lw3, a3, g3, k_k.reshape(1, D), k_a.reshape(1, D), r_k.reshape(1, D),
      ln_w.reshape(1, D), ln_b.reshape(1, D), onehot, onehot.T, s0)


def _mlstm_gate_epilogue(acc, bias):
    z = GATE_SOFTCAP * jnp.tanh((acc + bias) / GATE_SOFTCAP)
    lane = lax.broadcasted_iota(jnp.int32, z.shape, 1)
    return jnp.where(lane < MLSTM_HEADS, z, jax.nn.log_sigmoid(z))


def _ffn_ple(h, hn, p_pair, w_gate, w_up, w_down, g_ple, w_ple_gate, w_ple_proj, next_gain, *, split_out):
    d_ff = w_down.shape[0]
    act = _proj(hn, [w_gate, w_up], out_dtype=BF16, tm=1024, tn=512,
                epilogue=lambda g, u: jax.nn.silu(g) * u, name="swiglu_up")
    h, hn = _res(act, w_down.astype(BF16), h, g_ple, tm=512, tk=d_ff // 4, hn_dtype=BF16, name="swiglu_down")
    return _res(hn, w_ple_gate.astype(BF16), h, next_gain, tm=512, tk=w_ple_gate.shape[0], hn_dtype=F32,
                x2=p_pair, w2=w_ple_proj.astype(BF16), emit_h=split_out is None, split_out=split_out, name="ple")


def kernel(x_prompt, x_sample, state_mlstm_C, state_mlstm_n, state_mlstm_m, state_rwkv_S, state_rwkv_shift,
           p_prompt, p_sample, norm_mix, norm_ffn, norm_ple, norm_final, ffn_w_gate, ffn_w_up, ffn_w_down,
           ple_w_proj, ple_w_gate, mlstm_w_q, mlstm_w_k, mlstm_w_v, mlstm_w_igate, mlstm_b_igate,
           mlstm_w_fgate, mlstm_b_fgate, mlstm_w_ogate, mlstm_norm_w, mlstm_w_out, rwkv_mu, rwkv_w_r,
           rwkv_w_k, rwkv_w_v, rwkv_w_o, rwkv_w0, rwkv_w1, rwkv_w2, rwkv_a0, rwkv_a1, rwkv_a2, rwkv_g1,
           rwkv_g2, rwkv_k_k, rwkv_k_a, rwkv_r_k, rwkv_ln_w, rwkv_ln_b):
    D = D_MODEL
    B, T, _ = x_prompt.shape
    BS, TS, _ = x_sample.shape
    MP, MS = B * T, BS * TS
    M = MP + MS
    H, DK, DV = MLSTM_HEADS, MLSTM_DQK, MLSTM_DV
    RH, RN = RWKV_HEADS, RWKV_HEAD_DIM
    PD = p_prompt.shape[-1]

    x_pair = (x_prompt.reshape(MP, D), x_sample.reshape(MS, D))
    p_pairs = [(p_prompt[i].reshape(MP, PD), p_sample[i].reshape(MS, PD)) for i in range(2)]

    hn = _norm(x_pair, norm_mix[0], BF16)
    w_qkvo = jnp.concatenate([mlstm_w_q[0], mlstm_w_k[0], mlstm_w_v[0], mlstm_w_ogate[0]], axis=1).astype(BF16)
    col_scale = jnp.concatenate([jnp.ones((H * DK,), F32), jnp.full((H * DK,), DK ** -0.5, F32),
                                 jnp.ones((2 * H * DV,), F32)])
    qkvo = _proj(hn, [w_qkvo], out_dtype=BF16, tm=1024, tn=1024, extras=[col_scale],
                 epilogue=lambda a, s: a * s, name="mlstm_qkvo")
    w_gates = jnp.zeros((D, LANES), F32).at[:, :H].set(mlstm_w_igate[0]).at[:, H:2 * H].set(mlstm_w_fgate[0])
    b_gates = jnp.zeros((LANES,), F32).at[:H].set(mlstm_b_igate[0]).at[H:2 * H].set(mlstm_b_fgate[0])
    gates = _proj(hn, [w_gates.astype(BF16)], out_dtype=F32, tm=1024, tn=LANES, extras=[b_gates],
                  epilogue=_mlstm_gate_epilogue, name="mlstm_gates")

    LP = math.gcd(T, MLSTM_CHUNK)
    LS = math.gcd(TS, MLSTM_CHUNK)
    gp = gates.reshape(M // LP, LP, LANES)
    gs = gates[MP:].reshape(MS // LS, LS, LANES)
    hm_p, C_p, n_p, m_p = _mlstm(qkvo.reshape(M // LP, LP, -1), gp, gp.transpose(0, 2, 1), mlstm_norm_w[0],
                                 jnp.zeros((B, H, DK, DV), F32), jnp.zeros((B, H, DK), F32),
                                 jnp.zeros((B, H), F32), n_seq=B, nc=T // LP, L=LP, row0=0)
    hm_s, C_s, n_s, m_s = _mlstm(qkvo[MP:].reshape(MS // LS, LS, -1), gs, gs.transpose(0, 2, 1), mlstm_norm_w[0],
                                 state_mlstm_C[0], state_mlstm_n[0], state_mlstm_m[0],
                                 n_seq=BS, nc=TS // LS, L=LS, row0=0)
    h, hn = _res((hm_p.reshape(MP, D), hm_s.reshape(MS, D)), mlstm_w_out[0].astype(BF16), x_pair, norm_ffn[0],
                 tm=512, tk=D, hn_dtype=BF16, name="mlstm_out")
    h, xn = _ffn_ple(h, hn, p_pairs[0], ffn_w_gate[0], ffn_w_up[0], ffn_w_down[0], norm_ple[0], ple_w_gate[0],
                     ple_w_proj[0], norm_mix[1], split_out=None)

    shift_p = xn[T - 1:MP:T]
    shift_s = xn[MP + TS - 1::TS]
    tm_r = _row_tile(math.gcd(MP, MS), 512)
    shift = dict(tm=tm_r, t_prompt=T, t_sample=TS, n_prompt_tiles=MP // tm_r)
    sx = jnp.repeat(state_rwkv_shift[0], TS, axis=0)
    mu = rwkv_mu[0]
    w_rkv = jnp.concatenate([rwkv_w_r[0], rwkv_w_k[0], rwkv_w_v[0]], axis=1).astype(BF16)
    rkv = _rkv(xn, sx, jnp.stack([mu[0], mu[2], mu[3]]), w_rkv, tm=tm_r, tn=1024, shift=shift)

    rank =max(-(-w.shape[1] // LANES) * LANES for w in (rwkv_w1[0], rwkv_a1[0], rwkv_g1[0]))

    def pad_to(w1, w2):
        r = w1.shape[1]
        return (jnp.pad(w1, ((0, 0), (0, rank - r))).astype(BF16), jnp.pad(w2, ((0, rank - r), (0, 0))).astype(BF16))

    lora_w = [pad_to(rwkv_w1[0], rwkv_w2[0]), pad_to(rwkv_a1[0], rwkv_a2[0]), pad_to(rwkv_g1[0], rwkv_g2[0])]
    lw, aa, gg = _lora(xn, sx, jnp.stack([mu[1], mu[4], mu[5]]), jnp.stack([w[0] for w in lora_w]),
                       jnp.stack([w[1] for w in lora_w]),
                       jnp.stack([rwkv_w0[0], rwkv_a0[0], jnp.zeros((D,), F32)]), tm=tm_r, shift=shift)

    LRP = math.gcd(T, RWKV_CHUNK)
    LRS = math.gcd(TS, RWKV_CHUNK)

    def run_rwkv(L, n_seq, t_len, lo, s0):
        view = lambda x: x[lo:].reshape((x.shape[0] - lo) // L, L, x.shape[1])
        return _rwkv(view(rkv), view(lw), view(aa), view(gg), rwkv_k_k[0], rwkv_k_a[0], rwkv_r_k[0],
                     rwkv_ln_w[0], rwkv_ln_b[0], s0, n_seq=n_seq, nc=t_len // L, L=L, row0=0)

    y_p, S_p = run_rwkv(LRP, B, T, 0, jnp.zeros((B, RH, RN, RN), F32))
    y_s, S_s = run_rwkv(LRS, BS, TS, MP, state_rwkv_S[0])
    h, hn = _res((y_p.reshape(MP, D), y_s.reshape(MS, D)), rwkv_w_o[0].astype(BF16), h, norm_ffn[1],
                 tm=512, tk=D, hn_dtype=BF16, name="rwkv_out")
    y_prompt, y_sample = _ffn_ple(h, hn, p_pairs[1], ffn_w_gate[1], ffn_w_up[1], ffn_w_down[1], norm_ple[1],
                                  ple_w_gate[1], ple_w_proj[1], norm_final, split_out=MP)

    return (y_prompt.reshape(B, T, D), y_sample.reshape(BS, TS, D),
            C_p[None], n_p[None], m_p.reshape(1, B, H), S_p[None], shift_p[None],
            C_s[None], n_s[None], m_s.reshape(1, BS, H), S_s[None], shift_s[None])
```

```python
import functools
import math

import jax
import jax.numpy as jnp
from jax import lax
from jax.experimental import pallas as pl
from jax.experimental.pallas import tpu as pltpu

F32 = jnp.float32
BF16 = jnp.bfloat16

D_MODEL = 2048
MLSTM_HEADS = 8
MLSTM_DQK = 128
MLSTM_DV = 256
MLSTM_CHUNK = 256
GATE_SOFTCAP = 15.0
RWKV_HEAD_DIM = 64
RWKV_HEADS = 32
RWKV_PAIRS = RWKV_HEADS // 2
RWKV_CHUNK = 64
NORM_EPS = 1e-6
RWKV_GN_EPS = 64e-5
L2_EPS = 1e-12
LANES = 128
SUBLANES = 8
VMEM_LIMIT_BYTES = 56 * 1024 * 1024


def _params(*sem):
    return pltpu.CompilerParams(dimension_semantics=sem, vmem_limit_bytes=VMEM_LIMIT_BYTES)


def _row_tile(m, preferred):
    t = preferred
    while m % t:
        t -= LANES
    return t


def _rms(x, g):
    return x * lax.rsqrt(jnp.mean(x * x, axis=-1, keepdims=True) + NORM_EPS) * g


def _dot(a, b):
    return jnp.dot(a, b, preferred_element_type=F32)


def _dot_nt(a, b):
    return lax.dot_general(a, b, (((1,), (1,)), ((), ())), preferred_element_type=F32)


def _dot_tn(a, b):
    return lax.dot_general(a, b, (((0,), (0,)), ((), ())), preferred_element_type=F32)


def _split3(x):
    hi = x.astype(BF16)
    r1 = x - hi.astype(F32)
    mid = r1.astype(BF16)
    lo = (r1 - mid.astype(F32)).astype(BF16)
    return hi, mid, lo


def _dot_sel(x, sel):
    hi, mid, lo = _split3(x)
    return _dot(hi, sel) + _dot(mid, sel) + _dot(lo, sel)


def _sel_dot(sel, x):
    hi, mid, lo = _split3(x)
    return _dot(sel, hi) + _dot(sel, mid) + _dot(sel, lo)


def _sel_dot_nt(x, sel):
    hi, mid, lo = _split3(x)
    return _dot_nt(hi, sel) + _dot_nt(mid, sel) + _dot_nt(lo, sel)


def _row_specs(x, tm, cols, col_index):
    if isinstance(x, tuple):
        a, b = x
        na = a.shape[0] // tm
        return ([pl.BlockSpec((tm, cols), lambda i, *r: (jnp.minimum(i, na - 1), col_index(*r))),
                 pl.BlockSpec((tm, cols), lambda i, *r: (jnp.maximum(i - na, 0), col_index(*r)))], [a, b], na)
    return [pl.BlockSpec((tm, cols), lambda i, *r: (i, col_index(*r)))], [x], None


def _rows(x):
    return sum(a.shape[0] for a in x) if isinstance(x, tuple) else x.shape[0]


def _pair_row_tile(operands, preferred, extra_rows=()):
    halves = [a.shape[0] for x in operands if x is not None for a in (x if isinstance(x, tuple) else (x,))]
    return _row_tile(math.gcd(*halves, *extra_rows), preferred)


def _load_rows(refs, na, dtype=None):
    vals = [r[...] if dtype is None else r[...].astype(dtype) for r in refs]
    if na is None:
        return vals[0]
    return jnp.where(pl.program_id(0) < na, vals[0], vals[1])


def _store_rows(refs, na, val):
    if na is None:
        refs[0][...] = val.astype(refs[0].dtype)
        return

    @pl.when(pl.program_id(0) < na)
    def _():
        refs[0][...] = val.astype(refs[0].dtype)

    @pl.when(pl.program_id(0) >= na)
    def _():
        refs[1][...] = val.astype(refs[1].dtype)


_zero_col = lambda *r: 0


def _norm_kernel(*refs, na):
    nx = 1 if na is None else 2
    g_ref, o_ref = refs[nx], refs[nx + 1]
    o_ref[...] = _rms(_load_rows(refs[:nx], na), g_ref[...]).astype(o_ref.dtype)


def _norm(x, g, out_dtype, tm=512):
    m = _rows(x)
    d = g.shape[0]
    tm = _pair_row_tile([x], tm)
    x_specs, x_args, na = _row_specs(x, tm, d, _zero_col)
    return pl.pallas_call(
        functools.partial(_norm_kernel, na=na),
        grid=(m // tm,),
        in_specs=x_specs + [pl.BlockSpec((1, d), lambda i: (0, 0))],
        out_specs=pl.BlockSpec((tm, d), lambda i: (i, 0)),
        out_shape=jax.ShapeDtypeStruct((m, d), out_dtype),
        compiler_params=_params("parallel"),
        name="rmsnorm",
    )(*x_args, g.reshape(1, d))


def _proj_kernel(*refs, nw, ne, epilogue):
    x_ref = refs[0]
    ws = refs[1:1 + nw]
    es = refs[1 + nw:1 + nw + ne]
    o_ref = refs[1 + nw + ne]
    xb = x_ref[...]
    accs = [_dot(xb, w[...].astype(BF16)) for w in ws]
    o_ref[...] = epilogue(*accs, *[e[...] for e in es]).astype(o_ref.dtype)


def _proj(x, ws, *, out_dtype, tm, tn, extras=(), epilogue=lambda a: a, name="proj"):
    m, k = x.shape
    n = ws[0].shape[1]
    tm = _row_tile(m, tm)
    in_specs = ([pl.BlockSpec((tm, k), lambda i, j: (i, 0))]
                + [pl.BlockSpec((k, tn), lambda i, j: (0, j)) for _ in ws]
                + [pl.BlockSpec((1, tn), lambda i, j: (0, j)) for _ in extras])
    return pl.pallas_call(
        functools.partial(_proj_kernel, nw=len(ws), ne=len(extras), epilogue=epilogue),
        grid=(m // tm, n // tn),
        in_specs=in_specs,
        out_specs=pl.BlockSpec((tm, tn), lambda i, j: (i, j)),
        out_shape=jax.ShapeDtypeStruct((m, n), out_dtype),
        compiler_params=_params("parallel", "arbitrary"),
        name=name,
    )(x, *ws, *[e.reshape(1, n) for e in extras])


def _res_kernel(*refs, nk, gated, emit_h, na_x, na_res, na_x2, na_out):
    pos = [0]

    def take(n):
        out = refs[pos[0]:pos[0] + n]
        pos[0] += n
        return out

    width = lambda na: 1 if na is None else 2
    x_refs = take(width(na_x))
    (w_ref,) = take(1)
    res_refs = take(width(na_res))
    (g_ref,) = take(1)
    if gated:
        x2_refs = take(width(na_x2))
        (w2_ref,) = take(1)
    h_refs = take(1) if emit_h else ()
    hn_refs = take(width(na_out))
    acc_refs = take(1) if nk > 1 else ()
    k = pl.program_id(1)

    part = _dot(_load_rows(x_refs, na_x, BF16), w_ref[...])
    if nk > 1:
        acc_ref = acc_refs[0]

        @pl.when(k == 0)
        def _():
            acc_ref[...] = part

        @pl.when(k > 0)
        def _():
            acc_ref[...] += part

    @pl.when(k == nk - 1)
    def _():
        a = acc_refs[0][...] if nk > 1 else part
        res = _load_rows(res_refs, na_res)
        if gated:
            h = res + jax.nn.sigmoid(a) * _dot(_load_rows(x2_refs, na_x2, BF16), w2_ref[...])
        else:
            h = res + a
        if emit_h:
            h_refs[0][...] = h
        _store_rows(hn_refs, na_out, _rms(h, g_ref[...]))


def _res(x, w, res, gain, *, tm, tk, hn_dtype, x2=None, w2=None, emit_h=True, split_out=None, name="res"):
    m = _rows(x)
    kdim, d = w.shape
    tm = _pair_row_tile([x, res, x2], tm, () if split_out is None else (split_out, m - split_out))
    nk = kdim // tk
    gated = x2 is not None
    x_specs, x_args, na_x = _row_specs(x, tm, tk, lambda k: k)
    res_specs, res_args, na_res = _row_specs(res, tm, d, _zero_col)
    const = lambda i, k: (0, 0)
    w_mode = dict(pipeline_mode=pl.Buffered(1)) if nk == 1 else {}
    in_specs = (x_specs + [pl.BlockSpec((tk, d), lambda i, k: (k, 0), **w_mode)]
                + res_specs + [pl.BlockSpec((1, d), const)])
    args = x_args + [w] + res_args + [gain.reshape(1, d)]
    na_x2 = None
    if gated:
        k2 = w2.shape[0]
        x2_specs, x2_args, na_x2 = _row_specs(x2, tm, k2, _zero_col)
        in_specs += x2_specs + [pl.BlockSpec((k2, d), const, pipeline_mode=pl.Buffered(1))]
        args += x2_args + [w2]
    out_specs, out_shape = [], []
    if emit_h:
        out_specs.append(pl.BlockSpec((tm, d), lambda i, k: (i, 0)))
        out_shape.append(jax.ShapeDtypeStruct((m, d), F32))
    na_out = None
    if split_out is None:
        out_specs.append(pl.BlockSpec((tm, d), lambda i, k: (i, 0)))
        out_shape.append(jax.ShapeDtypeStruct((m, d), hn_dtype))
    else:
        na_out = split_out // tm
        out_specs += [pl.BlockSpec((tm, d), lambda i, k: (jnp.minimum(i, na_out - 1), 0)),
                      pl.BlockSpec((tm, d), lambda i, k: (jnp.maximum(i - na_out, 0), 0))]
        out_shape += [jax.ShapeDtypeStruct((split_out, d), hn_dtype),
                      jax.ShapeDtypeStruct((m - split_out, d), hn_dtype)]
    return pl.pallas_call(
        functools.partial(_res_kernel, nk=nk, gated=gated, emit_h=emit_h, na_x=na_x, na_res=na_res,
                          na_x2=na_x2, na_out=na_out),
        grid=(m // tm, nk),
        in_specs=in_specs,
        out_specs=out_specs,
        out_shape=out_shape,
        scratch_shapes=[pltpu.VMEM((tm, d), F32)] if nk > 1 else [],
        compiler_params=_params("parallel", "arbitrary"),
        name=name,
    )(*args)


def _mlstm_kernel(q_ref, k_ref, v_ref, og_ref, g_ref, gt_ref, nw_ref, c0_ref, n0_ref, m0_ref,
                  h_ref, c_ref, n_ref, m_ref, *, L):
    H, DK, DV = MLSTM_HEADS, MLSTM_DQK, MLSTM_DV

    @pl.when(pl.program_id(1) == 0)
    def _():
        c_ref[...] = c0_ref[...]
        n_ref[...] = n0_ref[...]
        m_ref[...] = m0_ref[...]

    q = q_ref[0].astype(BF16)
    k = k_ref[0].astype(BF16)
    v = v_ref[0].astype(BF16)
    og = og_ref[0]
    G = g_ref[0]
    GT = gt_ref[0]
    nw = nw_ref[...]
    m_prev = m_ref[0]

    row = lax.broadcasted_iota(jnp.int32, (L, L), 0)
    col = lax.broadcasted_iota(jnp.int32, (L, L), 1)
    causal = col <= row
    tri = causal.astype(BF16)
    b_col = _sel_dot(tri, G)
    b_row = _sel_dot_nt(GT, tri)
    lane_h = lax.broadcasted_iota(jnp.int32, (1, H), 1)

    hs = range(H)
    qh = [q[:, h * DK:(h + 1) * DK] for h in hs]
    kh = [k[:, h * DK:(h + 1) * DK] for h in hs]
    vh = [v[:, h * DV:(h + 1) * DV] for h in hs]
    ch = [c_ref[0, h] for h in hs]
    nh = [n_ref[0, h:h + 1, :] for h in hs]
    bc = [b_col[:, H + h:H + h + 1] for h in hs]
    li_c = [G[:, h:h + 1] for h in hs]
    m_h = [m_prev[:, h:h + 1] for h in hs]
    qk = [_dot_nt(qh[h], kh[h]) for h in hs]
    qc = [_dot(qh[h], ch[h].astype(BF16)) for h in hs]
    dlog = [jnp.where(causal, bc[h] - b_row[H + h:H + h + 1, :] + GT[h:h + 1, :], -jnp.inf) for h in hs]
    a = [bc[h] + m_h[h] for h in hs]
    m_t = [jnp.maximum(a[h], jnp.max(dlog[h], axis=1, keepdims=True)) for h in hs]
    w_inter = [jnp.exp(a[h] - m_t[h]) for h in hs]
    s = [qk[h] * jnp.exp(dlog[h] - m_t[h]) for h in hs]
    sv = [_dot(s[h].astype(BF16), vh[h]) for h in hs]
    m_new = [m_t[h][L - 1:L, :] for h in hs]
    b_last = [bc[h][L - 1:L, :] for h in hs]
    wk = [jnp.exp(b_last[h] - bc[h] + li_c[h] - m_new[h]) for h in hs]
    kv = [_dot_tn(kh[h], (vh[h].astype(F32) * wk[h]).astype(BF16)) for h in hs]
    decay = [jnp.exp(b_last[h] + m_h[h] - m_new[h]) for h in hs]
    for h in hs:
        c_ref[0, h] = decay[h] * ch[h] + kv[h]
        n_ref[0, h:h + 1, :] = decay[h] * nh[h] + jnp.sum(kh[h].astype(F32) * wk[h], axis=0, keepdims=True)
    m_out = jnp.zeros((1, H), F32)
    for h in hs:
        m_out = jnp.where(lane_h == h, m_new[h], m_out)
    m_ref[0] = m_out

    for h in hs:
        num = w_inter[h] * qc[h] + sv[h]
        den = (w_inter[h] * jnp.sum(qh[h].astype(F32) * nh[h], axis=1, keepdims=True)
               + jnp.sum(s[h], axis=1, keepdims=True))
        hh = num / jnp.maximum(jnp.abs(den), jnp.exp(-m_t[h]))
        hh = hh * lax.rsqrt(jnp.mean(hh * hh, axis=1, keepdims=True) + NORM_EPS)
        o = jax.nn.sigmoid(og[:, h * DV:(h + 1) * DV].astype(F32))
        h_ref[0, :, h * DV:(h + 1) * DV] = (o * (hh * nw[:, h * DV:(h + 1) * DV])).astype(h_ref.dtype)


def _mlstm(qkvo3, g3, gt3, nw, c0, n0, m0, *, n_seq, nc, L, row0, out_dtype):
    H, DK, DV = MLSTM_HEADS, MLSTM_DQK, MLSTM_DV
    nqk, nv = H * DK, H * DV

    def rows(colblk):
        return lambda s, c: (row0 + s * nc + c, 0, colblk)

    state = lambda s, c: (0, s, 0, 0, 0)
    state3 = lambda s, c: (s, 0, 0)
    in_specs = [
        pl.BlockSpec((1, L, nqk), rows(0)),
        pl.BlockSpec((1, L, nqk), rows(1)),
        pl.BlockSpec((1, L, nv), rows(1)),
        pl.BlockSpec((1, L, nv), rows(2)),
        pl.BlockSpec((1, L, LANES), rows(0)),
        pl.BlockSpec((1, LANES, L), rows(0)),
        pl.BlockSpec((1, nv), lambda s, c: (0, 0)),
        pl.BlockSpec((None, 1, H, DK, DV), state),
        pl.BlockSpec((1, H, DK), state3),
        pl.BlockSpec((1, 1, H), state3),
    ]
    out_specs = [
        pl.BlockSpec((1, L, nv), lambda s, c: (s * nc + c, 0, 0)),
        pl.BlockSpec((None, 1, H, DK, DV), state),
        pl.BlockSpec((1, H, DK), state3),
        pl.BlockSpec((1, 1, H), state3),
    ]
    out_shape = [
        jax.ShapeDtypeStruct((n_seq * nc, L, nv), out_dtype),
        jax.ShapeDtypeStruct((1, n_seq, H, DK, DV), F32),
        jax.ShapeDtypeStruct((n_seq, H, DK), F32),
        jax.ShapeDtypeStruct((n_seq, 1, H), F32),
    ]
    return pl.pallas_call(
        functools.partial(_mlstm_kernel, L=L),
        grid=(n_seq, nc),
        in_specs=in_specs,
        out_specs=out_specs,
        out_shape=out_shape,
        compiler_params=_params("parallel", "arbitrary"),
        name="mlstm_chunks",
    )(qkvo3, qkvo3, qkvo3, qkvo3, g3, gt3, nw.reshape(1, nv), c0, n0, m0.reshape(n_seq, 1, H))


def _shifted(x_ref, prev_ref, sx_ref, *, tm, t_prompt, t_sample, n_prompt_tiles):
    i = pl.program_id(0)
    x = x_ref[...]
    rolled = pltpu.roll(x, 1, axis=0)
    rowi = lax.broadcasted_iota(jnp.int32, (tm, 1), 0)
    first = jnp.where((i * tm) % t_prompt == 0, 0.0, prev_ref[SUBLANES - 1:SUBLANES, :])
    xprev = jnp.where(rowi == 0, first, rolled)
    at_sample_start = jnp.logical_and(i >= n_prompt_tiles, rowi % t_sample == 0)
    return x, jnp.where(at_sample_start, sx_ref[...], xprev)


def _shift_specs(tm, d, n_prompt_tiles):
    return [pl.BlockSpec((tm, d), lambda i, *r: (i, 0)),
            pl.BlockSpec((SUBLANES, d), lambda i, *r: (jnp.maximum(i * (tm // SUBLANES) - 1, 0), 0)),
            pl.BlockSpec((tm, d), lambda i, *r: (jnp.maximum(i - n_prompt_tiles, 0), 0))]


def _rkv_kernel(x_ref, prev_ref, sx_ref, mu_ref, w_ref, o_ref, xb_ref, *, tiles_per_group, shift):
    j = pl.program_id(1)

    @pl.when(j == 0)
    def _():
        x, xp = _shifted(x_ref, prev_ref, sx_ref, **shift)
        dx = xp - x
        for c in range(3):
            xb_ref[c] = (x + dx * mu_ref[c:c + 1, :]).astype(BF16)

    o_ref[...] = _dot(xb_ref[j // tiles_per_group], w_ref[...]).astype(o_ref.dtype)


def _rkv(xn, sx, mu3, w_rkv, *, tm, tn, shift):
    m, d = xn.shape
    n = w_rkv.shape[1]
    return pl.pallas_call(
        functools.partial(_rkv_kernel, tiles_per_group=d // tn, shift=shift),
        grid=(m // tm, n // tn),
        in_specs=_shift_specs(tm, d, shift["n_prompt_tiles"]) + [
            pl.BlockSpec((3, d), lambda i, j: (0, 0)), pl.BlockSpec((d, tn), lambda i, j: (0, j))],
        out_specs=pl.BlockSpec((tm, tn), lambda i, j: (i, j)),
        out_shape=jax.ShapeDtypeStruct((m, n), BF16),
        scratch_shapes=[pltpu.VMEM((3, tm, d), BF16)],
        compiler_params=_params("parallel", "arbitrary"),
        name="rwkv_rkv",
    )(xn, xn, sx, mu3, w_rkv)


def _lora_kernel(x_ref, prev_ref, sx_ref, mu_ref, w1_ref, w2_ref, b_ref, lw_ref, a_ref, g_ref, *, shift):
    x, xp = _shifted(x_ref, prev_ref, sx_ref, **shift)
    dx = xp - x

    def branch(c, mid):
        xm = (x + dx * mu_ref[c:c + 1, :]).astype(BF16)
        hid = mid(_dot(xm, w1_ref[c])).astype(BF16)
        return b_ref[c:c + 1, :] + _dot(hid, w2_ref[c])

    lw_ref[...] = -jnp.exp(-jax.nn.softplus(-branch(0, jnp.tanh)) - 0.5)
    a_ref[...] = jax.nn.sigmoid(branch(1, lambda z: z))
    g_ref[...] = branch(2, jax.nn.sigmoid).astype(g_ref.dtype)


def _lora(xn, sx, mu3, w1s, w2s, bias3, *, tm, shift):
    m, d = xn.shape
    r = w1s.shape[2]
    row = pl.BlockSpec((tm, d), lambda i: (i, 0))
    return pl.pallas_call(
        functools.partial(_lora_kernel, shift=shift),
        grid=(m // tm,),
        in_specs=_shift_specs(tm, d, shift["n_prompt_tiles"]) + [
            pl.BlockSpec((3, d), lambda i: (0, 0)), pl.BlockSpec((3, d, r), lambda i: (0, 0, 0)),
            pl.BlockSpec((3, r, d), lambda i: (0, 0, 0)), pl.BlockSpec((3, d), lambda i: (0, 0))],
        out_specs=[row, row, row],
        out_shape=[jax.ShapeDtypeStruct((m, d), F32), jax.ShapeDtypeStruct((m, d), F32),
                   jax.ShapeDtypeStruct((m, d), BF16)],
        compiler_params=_params("parallel"),
        name="rwkv_lora",
    )(xn, xn, sx, mu3, w1s, w2s, bias3)


def _rwkv_kernel(r_ref, k_ref, v_ref, lw_ref, a_ref, g_ref, kk_ref, ka_ref, rk_ref, lnw_ref, lnb_ref,
                 e_ref, et_ref, s0_ref, o_ref, s_ref, sbd_ref, y_ref, *, L, nc):
    N = RWKV_HEAD_DIM
    c = pl.program_id(1)
    lane = lax.broadcasted_iota(jnp.int32, (1, LANES), 1)
    head0 = lane < N

    @pl.when(c == 0)
    def _():
        z = jnp.zeros((N, N), F32)
        for p in range(RWKV_PAIRS):
            top = jnp.concatenate([s0_ref[0, 2 * p], z], axis=1)
            bot = jnp.concatenate([z, s0_ref[0, 2 * p + 1]], axis=1)
            sbd_ref[p] = jnp.concatenate([top, bot], axis=0)

    E = e_ref[...]
    ET = et_ref[...]

    def seg_sum_bcast(x):
        return _dot_sel(_dot_sel(x, E), ET)

    r = r_ref[0].astype(F32)
    k = k_ref[0].astype(F32)
    v = v_ref[0].astype(F32)
    lw = lw_ref[0]
    a = a_ref[0]
    kk = k * kk_ref[...]
    kk = kk / jnp.maximum(jnp.sqrt(seg_sum_bcast(kk * kk)), L2_EPS)
    k2 = k * (1.0 + (a - 1.0) * ka_ref[...])

    row = lax.broadcasted_iota(jnp.int32, (L, L), 0)
    col = lax.broadcasted_iota(jnp.int32, (L, L), 1)
    tri = (col <= row).astype(BF16)
    logp = _sel_dot(tri, lw)
    p_incl = jnp.exp(logp)
    inv_p = jnp.exp(-logp)
    at = -kk * jnp.exp(logp - lw)
    bt = kk * a * inv_p
    kt = k2 * inv_p
    rt = r * p_incl
    p_last = p_incl[L - 1:L, :]

    rows2 = 2 * L
    row2 = lax.broadcasted_iota(jnp.int32, (rows2, rows2), 0)
    col2 = lax.broadcasted_iota(jnp.int32, (rows2, rows2), 1)
    same = (row2 >= L) == (col2 >= L)
    strict = same & (col2 < row2)
    incl = same & (col2 <= row2)
    eye = (row2 == col2).astype(F32)
    n_sq = max(int(math.ceil(math.log2(L))) - 1, 0)

    def stack(x):
        return jnp.concatenate([jnp.where(head0, x, 0.0), jnp.where(head0, 0.0, x)], axis=0)

    pairs = range(RWKV_PAIRS)
    sls = [slice(p * LANES, (p + 1) * LANES) for p in pairs]
    xar = [jnp.concatenate([stack(at[:, s]), stack(rt[:, s])], axis=0).astype(BF16) for s in sls]
    ybk = [jnp.concatenate([stack(bt[:, s]), stack(kt[:, s])], axis=0).astype(BF16) for s in sls]
    vr = [stack(v[:, s]).astype(BF16) for s in sls]
    sbd = [sbd_ref[p] for p in pairs]
    sb = [x.astype(BF16) for x in sbd]

    if rows2 % LANES == 0:
        big = [_dot_nt(xar[p], ybk[p]) for p in pairs]
        blk = lambda p, i, j: big[p][i * rows2:(i + 1) * rows2, j * rows2:(j + 1) * rows2]
    else:
        blk = lambda p, i, j: _dot_nt(xar[p][i * rows2:(i + 1) * rows2], ybk[p][j * rows2:(j + 1) * rows2])
    n_ab = [jnp.where(strict, blk(p, 0, 0), 0.0) for p in pairs]
    a_akrk = [jnp.concatenate([jnp.where(strict, blk(p, 0, 1), 0.0), jnp.where(incl, blk(p, 1, 1), 0.0)],
                              axis=0).astype(BF16) for p in pairs]
    a_rb = [jnp.where(incl, blk(p, 1, 0), 0.0).astype(BF16) for p in pairs]

    t_inv = [eye + n for n in n_ab]
    pw = [n.astype(BF16) for n in n_ab]
    for _ in range(n_sq):
        pw = [_dot(x, x).astype(BF16) for x in pw]
        t_inv = [t + _dot(t.astype(BF16), x) for t, x in zip(t_inv, pw)]

    xs = [_dot_nt(xar[p], sb[p]) for p in pairs]
    av = [_dot(a_akrk[p], vr[p]) for p in pairs]
    ub = [_dot(t_inv[p].astype(BF16), (xs[p][:rows2] + av[p][:rows2]).astype(BF16)).astype(BF16) for p in pairs]
    yr = [xs[p][rows2:] + _dot(a_rb[p], ub[p]) + av[p][rows2:] for p in pairs]
    for p in pairs:
        y_ref[:, sls[p]] = yr[p][:L] + yr[p][L:]
    for p in pairs:
        upd = _dot_tn(jnp.concatenate([ub[p], vr[p]], axis=0), ybk[p])
        sbd_ref[p] = (sbd[p] + upd) * p_last[:, sls[p]]

    y = y_ref[...]
    mean = seg_sum_bcast(y) * (1.0 / N)
    yc = y - mean
    var = seg_sum_bcast(yc * yc) * (1.0 / N)
    yn = yc * lax.rsqrt(var + RWKV_GN_EPS) * lnw_ref[...] + lnb_ref[...]
    bonus = seg_sum_bcast(r * k2 * rk_ref[...]) * v
    o_ref[0] = ((yn + bonus) * g_ref[0].astype(F32)).astype(o_ref.dtype)

    @pl.when(c == nc - 1)
    def _():
        for p in range(RWKV_PAIRS):
            blk_p = sbd_ref[p]
            s_ref[0, 2 * p] = blk_p[:N, :N]
            s_ref[0, 2 * p + 1] = blk_p[N:, N:]


def _rwkv(rkv3, lw3, a3, g3, k_k, k_a, r_k, ln_w, ln_b, s0, *, n_seq, nc, L, row0, out_dtype):
    D, H, N = D_MODEL, RWKV_HEADS, RWKV_HEAD_DIM
    onehot = (jnp.arange(D)[:, None] // N == jnp.arange(H)[None, :]).astype(BF16)

    def rows(colblk):
        return lambda s, c: (row0 + s * nc + c, 0, colblk)

    vec = pl.BlockSpec((1, D), lambda s, c: (0, 0))
    state = lambda s, c: (0, s, 0, 0, 0)
    in_specs = [
        pl.BlockSpec((1, L, D), rows(0)), pl.BlockSpec((1, L, D), rows(1)), pl.BlockSpec((1, L, D), rows(2)),
        pl.BlockSpec((1, L, D), rows(0)), pl.BlockSpec((1, L, D), rows(0)), pl.BlockSpec((1, L, D), rows(0)),
        vec, vec, vec, vec, vec,
        pl.BlockSpec((D, H), lambda s, c: (0, 0)), pl.BlockSpec((H, D), lambda s, c: (0, 0)),
        pl.BlockSpec((None, 1, H, N, N), state),
    ]
    out_specs = [
        pl.BlockSpec((1, L, D), lambda s, c: (s * nc + c, 0, 0)),
        pl.BlockSpec((None, 1, H, N, N), state),
    ]
    out_shape = [
        jax.ShapeDtypeStruct((n_seq * nc, L, D), out_dtype),
        jax.ShapeDtypeStruct((1, n_seq, H, N, N), F32),
    ]
    return pl.pallas_call(
        functools.partial(_rwkv_kernel, L=L, nc=nc),
        grid=(n_seq, nc),
        in_specs=in_specs,
        out_specs=out_specs,
        out_shape=out_shape,
        scratch_shapes=[pltpu.VMEM((RWKV_PAIRS, LANES, LANES), F32), pltpu.VMEM((L, D), F32)],
        compiler_params=_params("parallel", "arbitrary"),
        name="rwkv7_chunks",
    )(rkv3, rkv3, rkv3, lw3, a3, g3, k_k.reshape(1, D), k_a.reshape(1, D), r_k.reshape(1, D),
      ln_w.reshape(1, D), ln_b.reshape(1, D), onehot, onehot.T, s0)


def _mlstm_gate_epilogue(acc, bias):
    z = GATE_SOFTCAP * jnp.tanh((acc + bias) / GATE_SOFTCAP)
    lane = lax.broadcasted_iota(jnp.int32, z.shape, 1)
    return jnp.where(lane < MLSTM_HEADS, z, jax.nn.log_sigmoid(z))


def _ffn_ple(h, hn, p_pair, w_gate, w_up, w_down, g_ple, w_ple_gate, w_ple_proj, next_gain, *, split_out):
    d_ff = w_down.shape[0]
    act = _proj(hn, [w_gate, w_up], out_dtype=BF16, tm=1024, tn=512,
                epilogue=lambda g, u: jax.nn.silu(g) * u, name="swiglu_up")
    h, hn = _res(act, w_down.astype(BF16), h, g_ple, tm=256, tk=d_ff, hn_dtype=BF16, name="swiglu_down")
    return _res(hn, w_ple_gate.astype(BF16), h, next_gain, tm=512, tk=w_ple_gate.shape[0], hn_dtype=F32,
                x2=p_pair, w2=w_ple_proj.astype(BF16), emit_h=split_out is None, split_out=split_out, name="ple")


def kernel(x_prompt, x_sample, state_mlstm_C, state_mlstm_n, state_mlstm_m, state_rwkv_S, state_rwkv_shift,
           p_prompt, p_sample, norm_mix, norm_ffn, norm_ple, norm_final, ffn_w_gate, ffn_w_up, ffn_w_down,
           ple_w_proj, ple_w_gate, mlstm_w_q, mlstm_w_k, mlstm_w_v, mlstm_w_igate, mlstm_b_igate,
           mlstm_w_fgate, mlstm_b_fgate, mlstm_w_ogate, mlstm_norm_w, mlstm_w_out, rwkv_mu, rwkv_w_r,
           rwkv_w_k, rwkv_w_v, rwkv_w_o, rwkv_w0, rwkv_w1, rwkv_w2, rwkv_a0, rwkv_a1, rwkv_a2, rwkv_g1,
           rwkv_g2, rwkv_k_k, rwkv_k_a, rwkv_r_k, rwkv_ln_w, rwkv_ln_b):
    D = D_MODEL
    B, T, _ = x_prompt.shape
    BS, TS, _ = x_sample.shape
    MP, MS = B * T, BS * TS
    M = MP + MS
    H, DK, DV = MLSTM_HEADS, MLSTM_DQK, MLSTM_DV
    RH, RN = RWKV_HEADS, RWKV_HEAD_DIM
    PD = p_prompt.shape[-1]

    x_pair = (x_prompt.reshape(MP, D), x_sample.reshape(MS, D))
    p_pairs = [(p_prompt[i].reshape(MP, PD), p_sample[i].reshape(MS, PD)) for i in range(2)]

    hn = _norm(x_pair, norm_mix[0], BF16)
    w_qkvo = jnp.concatenate([mlstm_w_q[0], mlstm_w_k[0], mlstm_w_v[0], mlstm_w_ogate[0]], axis=1).astype(BF16)
    col_scale = jnp.concatenate([jnp.ones((H * DK,), F32), jnp.full((H * DK,), DK ** -0.5, F32),
                                 jnp.ones((2 * H * DV,), F32)])
    qkvo = _proj(hn, [w_qkvo], out_dtype=BF16, tm=1024, tn=1024, extras=[col_scale],
                 epilogue=lambda a, s: a * s, name="mlstm_qkvo")
    w_gates = jnp.zeros((D, LANES), F32).at[:, :H].set(mlstm_w_igate[0]).at[:, H:2 * H].set(mlstm_w_fgate[0])
    b_gates = jnp.zeros((LANES,), F32).at[:H].set(mlstm_b_igate[0]).at[H:2 * H].set(mlstm_b_fgate[0])
    gates = _proj(hn, [w_gates.astype(BF16)], out_dtype=F32, tm=1024, tn=LANES, extras=[b_gates],
                  epilogue=_mlstm_gate_epilogue, name="mlstm_gates")

    LP = math.gcd(T, MLSTM_CHUNK)
    LS = math.gcd(TS, MLSTM_CHUNK)
    gp = gates.reshape(M // LP, LP, LANES)
    gs = gates[MP:].reshape(MS // LS, LS, LANES)
    hm_p, C_p, n_p, m_p = _mlstm(qkvo.reshape(M // LP, LP, -1), gp, gp.transpose(0, 2, 1), mlstm_norm_w[0],
                                 jnp.zeros((1, B, H, DK, DV), F32), jnp.zeros((B, H, DK), F32),
                                 jnp.zeros((B, H), F32), n_seq=B, nc=T // LP, L=LP, row0=0, out_dtype=BF16)
    hm_s, C_s, n_s, m_s = _mlstm(qkvo[MP:].astype(F32).reshape(MS // LS, LS, -1), gs, gs.transpose(0, 2, 1),
                                 mlstm_norm_w[0], state_mlstm_C, state_mlstm_n[0], state_mlstm_m[0],
                                 n_seq=BS, nc=TS // LS, L=LS, row0=0, out_dtype=F32)
    h, hn = _res((hm_p.reshape(MP, D), hm_s.reshape(MS, D)), mlstm_w_out[0].astype(BF16), x_pair, norm_ffn[0],
                 tm=512, tk=D, hn_dtype=BF16, name="mlstm_out")
    h, xn = _ffn_ple(h, hn, p_pairs[0], ffn_w_gate[0], ffn_w_up[0], ffn_w_down[0], norm_ple[0], ple_w_gate[0],
                     ple_w_proj[0], norm_mix[1], split_out=None)

    shift_p = xn[T - 1:MP:T]
    shift_s = xn[MP + TS - 1::TS]
    tm_r = _row_tile(math.gcd(MP, MS), 512)
    shift = dict(tm=tm_r, t_prompt=T, t_sample=TS, n_prompt_tiles=MP // tm_r)
    sx = jnp.repeat(state_rwkv_shift[0], TS, axis=0)
    mu = rwkv_mu[0]
    w_rkv = jnp.concatenate([rwkv_w_r[0], rwkv_w_k[0], rwkv_w_v[0]], axis=1).astype(BF16)
    rkv = _rkv(xn, sx, jnp.stack([mu[0], mu[2], mu[3]]), w_rkv, tm=tm_r, tn=1024, shift=shift)

    rank =max(-(-w.shape[1] // LANES) * LANES for w in (rwkv_w1[0], rwkv_a1[0], rwkv_g1[0]))

    def pad_to(w1, w2):
        r = w1.shape[1]
        return (jnp.pad(w1, ((0, 0), (0, rank - r))).astype(BF16), jnp.pad(w2, ((0, rank - r), (0, 0))).astype(BF16))

    lora_w = [pad_to(rwkv_w1[0], rwkv_w2[0]), pad_to(rwkv_a1[0], rwkv_a2[0]), pad_to(rwkv_g1[0], rwkv_g2[0])]
    lw, aa, gg = _lora(xn, sx, jnp.stack([mu[1], mu[4], mu[5]]), jnp.stack([w[0] for w in lora_w]),
                       jnp.stack([w[1] for w in lora_w]),
                       jnp.stack([rwkv_w0[0], rwkv_a0[0], jnp.zeros((D,), F32)]), tm=tm_r, shift=shift)

    LRP = math.gcd(T, RWKV_CHUNK)
    LRS = math.gcd(TS, RWKV_CHUNK)

    def run_rwkv(L, n_seq, t_len, lo, s0, dtype):
        view = lambda x: x[lo:].astype(F32 if dtype == F32 else x.dtype).reshape((x.shape[0] - lo) // L, L, x.shape[1])
        return _rwkv(view(rkv), view(lw), view(aa), view(gg), rwkv_k_k[0], rwkv_k_a[0], rwkv_r_k[0],
                     rwkv_ln_w[0], rwkv_ln_b[0], s0, n_seq=n_seq, nc=t_len // L, L=L, row0=0, out_dtype=dtype)

    y_p, S_p = run_rwkv(LRP, B, T, 0, jnp.zeros((1, B, RH, RN, RN), F32), BF16)
    y_s, S_s = run_rwkv(LRS, BS, TS, MP, state_rwkv_S, F32)
    h, hn = _res((y_p.reshape(MP, D), y_s.reshape(MS, D)), rwkv_w_o[0].astype(BF16), h, norm_ffn[1],
                 tm=512, tk=D, hn_dtype=BF16, name="rwkv_out")
    y_prompt, y_sample = _ffn_ple(h, hn, p_pairs[1], ffn_w_gate[1], ffn_w_up[1], ffn_w_down[1], norm_ple[1],
                                  ple_w_gate[1], ple_w_proj[1], norm_final, split_out=MP)

    return (y_prompt.reshape(B, T, D), y_sample.reshape(BS, TS, D),
            C_p, n_p[None], m_p.reshape(1, B, H), S_p, shift_p[None],
            C_s, n_s[None], m_s.reshape(1, BS, H), S_s, shift_s[None])
```

```python
import functools
import math

import jax
import jax.numpy as jnp
from jax import lax
from jax.experimental import pallas as pl
from jax.experimental.pallas import tpu as pltpu

F32 = jnp.float32
BF16 = jnp.bfloat16

D_MODEL = 2048
MLSTM_HEADS = 8
MLSTM_DQK = 128
MLSTM_DV = 256
MLSTM_CHUNK = 256
GATE_SOFTCAP = 15.0
RWKV_HEAD_DIM = 64
RWKV_HEADS = 32
RWKV_PAIRS = RWKV_HEADS // 2
RWKV_CHUNK = 64
NORM_EPS = 1e-6
RWKV_GN_EPS = 64e-5
L2_EPS = 1e-12
LANES = 128
SUBLANES = 8
VMEM_LIMIT_BYTES = 56 * 1024 * 1024


def _params(*sem):
    return pltpu.CompilerParams(dimension_semantics=sem, vmem_limit_bytes=VMEM_LIMIT_BYTES)


def _row_tile(m, preferred):
    t = preferred
    while m % t:
        t -= LANES
    return t


def _rms(x, g):
    return x * lax.rsqrt(jnp.mean(x * x, axis=-1, keepdims=True) + NORM_EPS) * g


def _dot(a, b):
    return jnp.dot(a, b, preferred_element_type=F32)


def _dot_nt(a, b):
    return lax.dot_general(a, b, (((1,), (1,)), ((), ())), preferred_element_type=F32)


def _dot_tn(a, b):
    return lax.dot_general(a, b, (((0,), (0,)), ((), ())), preferred_element_type=F32)


def _split3(x):
    hi = x.astype(BF16)
    r1 = x - hi.astype(F32)
    mid = r1.astype(BF16)
    lo = (r1 - mid.astype(F32)).astype(BF16)
    return hi, mid, lo


def _dot_sel(x, sel):
    hi, mid, lo = _split3(x)
    return _dot(hi, sel) + _dot(mid, sel) + _dot(lo, sel)


def _sel_dot(sel, x):
    hi, mid, lo = _split3(x)
    return _dot(sel, hi) + _dot(sel, mid) + _dot(sel, lo)


def _sel_dot_nt(x, sel):
    hi, mid, lo = _split3(x)
    return _dot_nt(hi, sel) + _dot_nt(mid, sel) + _dot_nt(lo, sel)


def _row_specs(x, tm, cols, col_index):
    if isinstance(x, tuple):
        a, b = x
        na = a.shape[0] // tm
        return ([pl.BlockSpec((tm, cols), lambda i, *r: (jnp.minimum(i, na - 1), col_index(*r))),
                 pl.BlockSpec((tm, cols), lambda i, *r: (jnp.maximum(i - na, 0), col_index(*r)))], [a, b], na)
    return [pl.BlockSpec((tm, cols), lambda i, *r: (i, col_index(*r)))], [x], None


def _rows(x):
    return sum(a.shape[0] for a in x) if isinstance(x, tuple) else x.shape[0]


def _pair_row_tile(operands, preferred, extra_rows=()):
    halves = [a.shape[0] for x in operands if x is not None for a in (x if isinstance(x, tuple) else (x,))]
    return _row_tile(math.gcd(*halves, *extra_rows), preferred)


def _load_rows(refs, na, dtype=None):
    vals = [r[...] if dtype is None else r[...].astype(dtype) for r in refs]
    if na is None:
        return vals[0]
    return jnp.where(pl.program_id(0) < na, vals[0], vals[1])


def _store_rows(refs, na, val):
    if na is None:
        refs[0][...] = val.astype(refs[0].dtype)
        return

    @pl.when(pl.program_id(0) < na)
    def _():
        refs[0][...] = val.astype(refs[0].dtype)

    @pl.when(pl.program_id(0) >= na)
    def _():
        refs[1][...] = val.astype(refs[1].dtype)


_zero_col = lambda *r: 0


def _norm_kernel(*refs, na):
    nx = 1 if na is None else 2
    g_ref, o_ref = refs[nx], refs[nx + 1]
    o_ref[...] = _rms(_load_rows(refs[:nx], na), g_ref[...]).astype(o_ref.dtype)


def _norm(x, g, out_dtype, tm=512):
    m = _rows(x)
    d = g.shape[0]
    tm = _pair_row_tile([x], tm)
    x_specs, x_args, na = _row_specs(x, tm, d, _zero_col)
    return pl.pallas_call(
        functools.partial(_norm_kernel, na=na),
        grid=(m // tm,),
        in_specs=x_specs + [pl.BlockSpec((1, d), lambda i: (0, 0))],
        out_specs=pl.BlockSpec((tm, d), lambda i: (i, 0)),
        out_shape=jax.ShapeDtypeStruct((m, d), out_dtype),
        compiler_params=_params("parallel"),
        name="rmsnorm",
    )(*x_args, g.reshape(1, d))


def _proj_kernel(*refs, nw, ne, epilogue):
    x_ref = refs[0]
    ws = refs[1:1 + nw]
    es = refs[1 + nw:1 + nw + ne]
    o_ref = refs[1 + nw + ne]
    xb = x_ref[...]
    accs = [_dot(xb, w[...].astype(BF16)) for w in ws]
    o_ref[...] = epilogue(*accs, *[e[...] for e in es]).astype(o_ref.dtype)


def _proj(x, ws, *, out_dtype, tm, tn, extras=(), epilogue=lambda a: a, layer=None, name="proj"):
    m, k = x.shape
    n = ws[0].shape[-1]
    tm = _row_tile(m, tm)
    if layer is None:
        w_spec = pl.BlockSpec((k, tn), lambda i, j: (0, j))
    else:
        w_spec = pl.BlockSpec((None, k, tn), lambda i, j: (layer, 0, j))
    in_specs = ([pl.BlockSpec((tm, k), lambda i, j: (i, 0))]
                + [w_spec for _ in ws]
                + [pl.BlockSpec((1, tn), lambda i, j: (0, j)) for _ in extras])
    return pl.pallas_call(
        functools.partial(_proj_kernel, nw=len(ws), ne=len(extras), epilogue=epilogue),
        grid=(m // tm, n // tn),
        in_specs=in_specs,
        out_specs=pl.BlockSpec((tm, tn), lambda i, j: (i, j)),
        out_shape=jax.ShapeDtypeStruct((m, n), out_dtype),
        compiler_params=_params("parallel", "arbitrary"),
        name=name,
    )(x, *ws, *[e.reshape(1, n) for e in extras])


def _res_kernel(*refs, nk, gated, emit_h, na_x, na_res, na_x2, na_out):
    pos = [0]

    def take(n):
        out = refs[pos[0]:pos[0] + n]
        pos[0] += n
        return out

    width = lambda na: 1 if na is None else 2
    x_refs = take(width(na_x))
    (w_ref,) = take(1)
    res_refs = take(width(na_res))
    (g_ref,) = take(1)
    if gated:
        x2_refs = take(width(na_x2))
        (w2_ref,) = take(1)
    h_refs = take(1) if emit_h else ()
    hn_refs = take(width(na_out))
    acc_refs = take(1) if nk > 1 else ()
    k = pl.program_id(1)

    part = _dot(_load_rows(x_refs, na_x, BF16), w_ref[...])
    if nk > 1:
        acc_ref = acc_refs[0]

        @pl.when(k == 0)
        def _():
            acc_ref[...] = part

        @pl.when(k > 0)
        def _():
            acc_ref[...] += part

    @pl.when(k == nk - 1)
    def _():
        a = acc_refs[0][...] if nk > 1 else part
        res = _load_rows(res_refs, na_res)
        if gated:
            h = res + jax.nn.sigmoid(a) * _dot(_load_rows(x2_refs, na_x2, BF16), w2_ref[...])
        else:
            h = res + a
        if emit_h:
            h_refs[0][...] = h
        _store_rows(hn_refs, na_out, _rms(h, g_ref[...]))


def _res(x, w, res, gain, *, tm, tk, hn_dtype, x2=None, w2=None, emit_h=True, split_out=None, layer=None,
         name="res"):
    m = _rows(x)
    kdim, d = w.shape[-2:]
    lead = () if layer is None else (None,)
    at_layer = (lambda *idx: idx) if layer is None else (lambda *idx: (layer,) + idx)
    tm = _pair_row_tile([x, res, x2], tm, () if split_out is None else (split_out, m - split_out))
    nk = kdim // tk
    gated = x2 is not None
    x_specs, x_args, na_x = _row_specs(x, tm, tk, lambda k: k)
    res_specs, res_args, na_res = _row_specs(res, tm, d, _zero_col)
    const = lambda i, k: (0, 0)
    w_mode = dict(pipeline_mode=pl.Buffered(1)) if nk == 1 else {}
    in_specs = (x_specs + [pl.BlockSpec(lead + (tk, d), lambda i, k: at_layer(k, 0), **w_mode)]
                + res_specs + [pl.BlockSpec((1, d), const)])
    args = x_args + [w] + res_args + [gain.reshape(1, d)]
    na_x2 = None
    if gated:
        k2 = w2.shape[-2]
        x2_specs, x2_args, na_x2 = _row_specs(x2, tm, k2, _zero_col)
        in_specs += x2_specs + [pl.BlockSpec(lead + (k2, d), lambda i, k: at_layer(0, 0),
                                             pipeline_mode=pl.Buffered(1))]
        args += x2_args + [w2]
    out_specs, out_shape = [], []
    if emit_h:
        out_specs.append(pl.BlockSpec((tm, d), lambda i, k: (i, 0)))
        out_shape.append(jax.ShapeDtypeStruct((m, d), F32))
    na_out = None
    if split_out is None:
        out_specs.append(pl.BlockSpec((tm, d), lambda i, k: (i, 0)))
        out_shape.append(jax.ShapeDtypeStruct((m, d), hn_dtype))
    else:
        na_out = split_out // tm
        out_specs += [pl.BlockSpec((tm, d), lambda i, k: (jnp.minimum(i, na_out - 1), 0)),
                      pl.BlockSpec((tm, d), lambda i, k: (jnp.maximum(i - na_out, 0), 0))]
        out_shape += [jax.ShapeDtypeStruct((split_out, d), hn_dtype),
                      jax.ShapeDtypeStruct((m - split_out, d), hn_dtype)]
    return pl.pallas_call(
        functools.partial(_res_kernel, nk=nk, gated=gated, emit_h=emit_h, na_x=na_x, na_res=na_res,
                          na_x2=na_x2, na_out=na_out),
        grid=(m // tm, nk),
        in_specs=in_specs,
        out_specs=out_specs,
        out_shape=out_shape,
        scratch_shapes=[pltpu.VMEM((tm, d), F32)] if nk > 1 else [],
        compiler_params=_params("parallel", "arbitrary"),
        name=name,
    )(*args)


def _mlstm_kernel(q_ref, k_ref, v_ref, og_ref, g_ref, gt_ref, nw_ref, c0_ref, n0_ref, m0_ref,
                  h_ref, c_ref, n_ref, m_ref, *, L):
    H, DK, DV = MLSTM_HEADS, MLSTM_DQK, MLSTM_DV

    @pl.when(pl.program_id(1) == 0)
    def _():
        c_ref[...] = c0_ref[...]
        n_ref[...] = n0_ref[...]
        m_ref[...] = m0_ref[...]

    q = q_ref[0].astype(BF16)
    k = k_ref[0].astype(BF16)
    v = v_ref[0].astype(BF16)
    og = og_ref[0]
    G = g_ref[0]
    GT = gt_ref[0]
    nw = nw_ref[...]
    m_prev = m_ref[0]

    row = lax.broadcasted_iota(jnp.int32, (L, L), 0)
    col = lax.broadcasted_iota(jnp.int32, (L, L), 1)
    causal = col <= row
    tri = causal.astype(BF16)
    b_col = _sel_dot(tri, G)
    b_row = _sel_dot_nt(GT, tri)
    lane_h = lax.broadcasted_iota(jnp.int32, (1, H), 1)

    hs = range(H)
    qh = [q[:, h * DK:(h + 1) * DK] for h in hs]
    kh = [k[:, h * DK:(h + 1) * DK] for h in hs]
    vh = [v[:, h * DV:(h + 1) * DV] for h in hs]
    ch = [c_ref[0, h] for h in hs]
    nh = [n_ref[0, h:h + 1, :] for h in hs]
    bc = [b_col[:, H + h:H + h + 1] for h in hs]
    li_c = [G[:, h:h + 1] for h in hs]
    m_h = [m_prev[:, h:h + 1] for h in hs]
    qk = [_dot_nt(qh[h], kh[h]) for h in hs]
    qc = [_dot(qh[h], ch[h].astype(BF16)) for h in hs]
    dlog = [jnp.where(causal, bc[h] - b_row[H + h:H + h + 1, :] + GT[h:h + 1, :], -jnp.inf) for h in hs]
    a = [bc[h] + m_h[h] for h in hs]
    m_t = [jnp.maximum(a[h], jnp.max(dlog[h], axis=1, keepdims=True)) for h in hs]
    w_inter = [jnp.exp(a[h] - m_t[h]) for h in hs]
    s = [qk[h] * jnp.exp(dlog[h] - m_t[h]) for h in hs]
    sv = [_dot(s[h].astype(BF16), vh[h]) for h in hs]
    m_new = [m_t[h][L - 1:L, :] for h in hs]
    b_last = [bc[h][L - 1:L, :] for h in hs]
    wk = [jnp.exp(b_last[h] - bc[h] + li_c[h] - m_new[h]) for h in hs]
    kv = [_dot_tn(kh[h], (vh[h].astype(F32) * wk[h]).astype(BF16)) for h in hs]
    decay = [jnp.exp(b_last[h] + m_h[h] - m_new[h]) for h in hs]
    for h in hs:
        c_ref[0, h] = decay[h] * ch[h] + kv[h]
        n_ref[0, h:h + 1, :] = decay[h] * nh[h] + jnp.sum(kh[h].astype(F32) * wk[h], axis=0, keepdims=True)
    m_out = jnp.zeros((1, H), F32)
    for h in hs:
        m_out = jnp.where(lane_h == h, m_new[h], m_out)
    m_ref[0] = m_out

    for h in hs:
        num = w_inter[h] * qc[h] + sv[h]
        den = (w_inter[h] * jnp.sum(qh[h].astype(F32) * nh[h], axis=1, keepdims=True)
               + jnp.sum(s[h], axis=1, keepdims=True))
        hh = num / jnp.maximum(jnp.abs(den), jnp.exp(-m_t[h]))
        hh = hh * lax.rsqrt(jnp.mean(hh * hh, axis=1, keepdims=True) + NORM_EPS)
        o = jax.nn.sigmoid(og[:, h * DV:(h + 1) * DV].astype(F32))
        h_ref[0, :, h * DV:(h + 1) * DV] = (o * (hh * nw[:, h * DV:(h + 1) * DV])).astype(h_ref.dtype)


def _mlstm(qkvo3, g3, gt3, nw, c0, n0, m0, *, n_seq, nc, L, row0, out_dtype):
    H, DK, DV = MLSTM_HEADS, MLSTM_DQK, MLSTM_DV
    nqk, nv = H * DK, H * DV

    def rows(colblk):
        return lambda s, c: (row0 + s * nc + c, 0, colblk)

    state = lambda s, c: (0, s, 0, 0, 0)
    state3 = lambda s, c: (s, 0, 0)
    in_specs = [
        pl.BlockSpec((1, L, nqk), rows(0)),
        pl.BlockSpec((1, L, nqk), rows(1)),
        pl.BlockSpec((1, L, nv), rows(1)),
        pl.BlockSpec((1, L, nv), rows(2)),
        pl.BlockSpec((1, L, LANES), rows(0)),
        pl.BlockSpec((1, LANES, L), rows(0)),
        pl.BlockSpec((1, nv), lambda s, c: (0, 0)),
        pl.BlockSpec((None, 1, H, DK, DV), state),
        pl.BlockSpec((1, H, DK), state3),
        pl.BlockSpec((1, 1, H), state3),
    ]
    out_specs = [
        pl.BlockSpec((1, L, nv), lambda s, c: (s * nc + c, 0, 0)),
        pl.BlockSpec((None, 1, H, DK, DV), state),
        pl.BlockSpec((1, H, DK), state3),
        pl.BlockSpec((1, 1, H), state3),
    ]
    out_shape = [
        jax.ShapeDtypeStruct((n_seq * nc, L, nv), out_dtype),
        jax.ShapeDtypeStruct((1, n_seq, H, DK, DV), F32),
        jax.ShapeDtypeStruct((n_seq, H, DK), F32),
        jax.ShapeDtypeStruct((n_seq, 1, H), F32),
    ]
    return pl.pallas_call(
        functools.partial(_mlstm_kernel, L=L),
        grid=(n_seq, nc),
        in_specs=in_specs,
        out_specs=out_specs,
        out_shape=out_shape,
        compiler_params=_params("parallel", "arbitrary"),
        name="mlstm_chunks",
    )(qkvo3, qkvo3, qkvo3, qkvo3, g3, gt3, nw.reshape(1, nv), c0, n0, m0.reshape(n_seq, 1, H))


def _shifted(x_ref, prev_ref, sx_ref, *, tm, t_prompt, t_sample, n_prompt_tiles):
    i = pl.program_id(0)
    x = x_ref[...]
    rolled = pltpu.roll(x, 1, axis=0)
    rowi = lax.broadcasted_iota(jnp.int32, (tm, 1), 0)
    first = jnp.where((i * tm) % t_prompt == 0, 0.0, prev_ref[SUBLANES - 1:SUBLANES, :])
    xprev = jnp.where(rowi == 0, first, rolled)
    at_sample_start = jnp.logical_and(i >= n_prompt_tiles, rowi % t_sample == 0)
    return x, jnp.where(at_sample_start, sx_ref[...], xprev)


def _shift_specs(tm, d, n_prompt_tiles):
    return [pl.BlockSpec((tm, d), lambda i, *r: (i, 0)),
            pl.BlockSpec((SUBLANES, d), lambda i, *r: (jnp.maximum(i * (tm // SUBLANES) - 1, 0), 0)),
            pl.BlockSpec((tm, d), lambda i, *r: (jnp.maximum(i - n_prompt_tiles, 0), 0))]


def _rkv_kernel(x_ref, prev_ref, sx_ref, mu_ref, w_ref, o_ref, xb_ref, *, tiles_per_group, shift):
    j = pl.program_id(1)

    @pl.when(j == 0)
    def _():
        x, xp = _shifted(x_ref, prev_ref, sx_ref, **shift)
        dx = xp - x
        for c in range(3):
            xb_ref[c] = (x + dx * mu_ref[c:c + 1, :]).astype(BF16)

    o_ref[...] = _dot(xb_ref[j // tiles_per_group], w_ref[...]).astype(o_ref.dtype)


def _rkv(xn, sx, mu3, w_rkv, *, tm, tn, shift):
    m, d = xn.shape
    n = w_rkv.shape[1]
    return pl.pallas_call(
        functools.partial(_rkv_kernel, tiles_per_group=d // tn, shift=shift),
        grid=(m // tm, n // tn),
        in_specs=_shift_specs(tm, d, shift["n_prompt_tiles"]) + [
            pl.BlockSpec((3, d), lambda i, j: (0, 0)), pl.BlockSpec((d, tn), lambda i, j: (0, j))],
        out_specs=pl.BlockSpec((tm, tn), lambda i, j: (i, j)),
        out_shape=jax.ShapeDtypeStruct((m, n), BF16),
        scratch_shapes=[pltpu.VMEM((3, tm, d), BF16)],
        compiler_params=_params("parallel", "arbitrary"),
        name="rwkv_rkv",
    )(xn, xn, sx, mu3, w_rkv)


def _lora_kernel(x_ref, prev_ref, sx_ref, mu_ref, w1_ref, w2_ref, b_ref, lw_ref, a_ref, g_ref, last_ref, *,
                 shift):
    x, xp = _shifted(x_ref, prev_ref, sx_ref, **shift)
    dx = xp - x
    n_last, tm = last_ref.shape[0], x.shape[0]
    pick = (lax.broadcasted_iota(jnp.int32, (n_last, tm), 1)
            == lax.broadcasted_iota(jnp.int32, (n_last, tm), 0) * (tm // n_last) + (tm // n_last - 1))
    last_ref[...] = _sel_dot(pick.astype(BF16), x)

    def branch(c, mid):
        xm = (x + dx * mu_ref[c:c + 1, :]).astype(BF16)
        hid = mid(_dot(xm, w1_ref[c])).astype(BF16)
        return b_ref[c:c + 1, :] + _dot(hid, w2_ref[c])

    lw_ref[...] = -jnp.exp(-jax.nn.softplus(-branch(0, jnp.tanh)) - 0.5)
    a_ref[...] = jax.nn.sigmoid(branch(1, lambda z: z))
    g_ref[...] = branch(2, jax.nn.sigmoid).astype(g_ref.dtype)


def _lora(xn, sx, mu3, w1s, w2s, bias3, *, tm, shift):
    m, d = xn.shape
    r = w1s.shape[2]
    row = pl.BlockSpec((tm, d), lambda i: (i, 0))
    return pl.pallas_call(
        functools.partial(_lora_kernel, shift=shift),
        grid=(m // tm,),
        in_specs=_shift_specs(tm, d, shift["n_prompt_tiles"]) + [
            pl.BlockSpec((3, d), lambda i: (0, 0)), pl.BlockSpec((3, d, r), lambda i: (0, 0, 0)),
            pl.BlockSpec((3, r, d), lambda i: (0, 0, 0)), pl.BlockSpec((3, d), lambda i: (0, 0))],
        out_specs=[row, row, row, pl.BlockSpec((tm // shift["t_sample"], d), lambda i: (i, 0))],
        out_shape=[jax.ShapeDtypeStruct((m, d), F32), jax.ShapeDtypeStruct((m, d), F32),
                   jax.ShapeDtypeStruct((m, d), BF16), jax.ShapeDtypeStruct((m // shift["t_sample"], d), F32)],
        compiler_params=_params("parallel"),
        name="rwkv_lora",
    )(xn, xn, sx, mu3, w1s, w2s, bias3)


def _rwkv_kernel(r_ref, k_ref, v_ref, lw_ref, a_ref, g_ref, kk_ref, ka_ref, rk_ref, lnw_ref, lnb_ref,
                 e_ref, et_ref, s0_ref, o_ref, s_ref, sbd_ref, y_ref, *, L, nb, nc):
    N = RWKV_HEAD_DIM
    R = nb * L
    R2 = 2 * R
    c = pl.program_id(1)
    lane = lax.broadcasted_iota(jnp.int32, (1, LANES), 1)
    head0 = lane < N
    seqs = range(nb)
    pairs = range(RWKV_PAIRS)

    @pl.when(c == 0)
    def _():
        z = jnp.zeros((N, N), F32)
        for j in seqs:
            for p in pairs:
                top = jnp.concatenate([s0_ref[j, 2 * p], z], axis=1)
                bot = jnp.concatenate([z, s0_ref[j, 2 * p + 1]], axis=1)
                sbd_ref[j, p] = jnp.concatenate([top, bot], axis=0)

    E = e_ref[...]
    ET = et_ref[...]

    def seg_sum_bcast(x):
        return _dot_sel(_dot_sel(x, E), ET)

    r = r_ref[...].astype(F32)
    k = k_ref[...].astype(F32)
    v = v_ref[...].astype(F32)
    lw = lw_ref[...]
    a = a_ref[...]
    kk = k * kk_ref[...]
    kk = kk / jnp.maximum(jnp.sqrt(seg_sum_bcast(kk * kk)), L2_EPS)
    k2 = k * (1.0 + (a - 1.0) * ka_ref[...])

    row = lax.broadcasted_iota(jnp.int32, (R, R), 0)
    col = lax.broadcasted_iota(jnp.int32, (R, R), 1)
    tri = ((col <= row) & (row // L == col // L)).astype(BF16)
    logp = _sel_dot(tri, lw)
    p_incl = jnp.exp(logp)
    inv_p = jnp.exp(-logp)
    at = -kk * jnp.exp(logp - lw)
    bt = kk * a * inv_p
    kt = k2 * inv_p
    rt = r * p_incl

    row2 = lax.broadcasted_iota(jnp.int32, (R2, R2), 0)
    col2 = lax.broadcasted_iota(jnp.int32, (R2, R2), 1)
    same = row2 // L == col2 // L
    strict = same & (col2 < row2)
    incl = same & (col2 <= row2)
    eye = (row2 == col2).astype(F32)
    n_sq = max(int(math.ceil(math.log2(L))) - 1, 0)

    def stack(x):
        parts = []
        for j in seqs:
            xj = x[j * L:(j + 1) * L]
            parts += [jnp.where(head0, xj, 0.0), jnp.where(head0, 0.0, xj)]
        return jnp.concatenate(parts, axis=0)

    sls = [slice(p * LANES, (p + 1) * LANES) for p in pairs]
    seq_rows = [slice(j * 2 * L, (j + 1) * 2 * L) for j in seqs]
    xa = [stack(at[:, s]).astype(BF16) for s in sls]
    xr = [stack(rt[:, s]).astype(BF16) for s in sls]
    yb = [stack(bt[:, s]).astype(BF16) for s in sls]
    yk = [stack(kt[:, s]).astype(BF16) for s in sls]
    vr = [stack(v[:, s]).astype(BF16) for s in sls]
    sbd = [[sbd_ref[j, p] for p in pairs] for j in seqs]
    sb = [[x.astype(BF16) for x in sj] for sj in sbd]

    big = [_dot_nt(jnp.concatenate([xa[p], xr[p]], axis=0), jnp.concatenate([yb[p], yk[p]], axis=0))
           for p in pairs]
    n_ab = [jnp.where(strict, big[p][:R2, :R2], 0.0) for p in pairs]
    a_akrk = [jnp.concatenate([jnp.where(strict, big[p][:R2, R2:], 0.0), jnp.where(incl, big[p][R2:, R2:], 0.0)],
                              axis=0).astype(BF16) for p in pairs]
    a_rb = [jnp.where(incl, big[p][R2:, :R2], 0.0).astype(BF16) for p in pairs]

    t_inv = [eye + n for n in n_ab]
    pw = [n.astype(BF16) for n in n_ab]
    for _ in range(n_sq):
        pw = [_dot(x, x).astype(BF16) for x in pw]
        t_inv = [t + _dot(t.astype(BF16), x) for t, x in zip(t_inv, pw)]

    xs = [[_dot_nt(jnp.concatenate([xa[p][seq_rows[j]], xr[p][seq_rows[j]]], axis=0), sb[j][p]) for j in seqs]
          for p in pairs]
    xsa = [jnp.concatenate([xs[p][j][:2 * L] for j in seqs], axis=0) for p in pairs]
    xsr = [jnp.concatenate([xs[p][j][2 * L:] for j in seqs], axis=0) for p in pairs]
    av = [_dot(a_akrk[p], vr[p]) for p in pairs]
    ub = [_dot(t_inv[p].astype(BF16), (xsa[p] + av[p][:R2]).astype(BF16)).astype(BF16) for p in pairs]
    yr = [xsr[p] + _dot(a_rb[p], ub[p]) + av[p][R2:] for p in pairs]
    for p in pairs:
        for j in seqs:
            lo = j * 2 * L
            y_ref[j * L:(j + 1) * L, sls[p]] = yr[p][lo:lo + L] + yr[p][lo + L:lo + 2 * L]
    for j in seqs:
        p_last = p_incl[(j + 1) * L - 1:(j + 1) * L, :]
        for p in pairs:
            upd = _dot_tn(jnp.concatenate([ub[p][seq_rows[j]], vr[p][seq_rows[j]]], axis=0),
                          jnp.concatenate([yb[p][seq_rows[j]], yk[p][seq_rows[j]]], axis=0))
            sbd_ref[j, p] = (sbd[j][p] + upd) * p_last[:, sls[p]]

    y = y_ref[...]
    mean = seg_sum_bcast(y) * (1.0 / N)
    yc = y - mean
    var = seg_sum_bcast(yc * yc) * (1.0 / N)
    yn = yc * lax.rsqrt(var + RWKV_GN_EPS) * lnw_ref[...] + lnb_ref[...]
    bonus = seg_sum_bcast(r * k2 * rk_ref[...]) * v
    o_ref[...] = ((yn + bonus) * g_ref[...].astype(F32)).astype(o_ref.dtype)

    @pl.when(c == nc - 1)
    def _():
        for j in seqs:
            for p in pairs:
                blk_p = sbd_ref[j, p]
                s_ref[j, 2 * p] = blk_p[:N, :N]
                s_ref[j, 2 * p + 1] = blk_p[N:, N:]


def _rwkv(rkv, lw, aa, gg, k_k, k_a, r_k, ln_w, ln_b, s0, *, n_seq, nb, nc, L, row0):
    D, H, N = D_MODEL, RWKV_HEADS, RWKV_HEAD_DIM
    R = nb * L
    assert nb == 1 or nc == 1
    onehot = (jnp.arange(D)[:, None] // N == jnp.arange(H)[None, :]).astype(BF16)

    def rows(colblk):
        return lambda s, c: (row0 + s * nc + c, colblk)

    vec = pl.BlockSpec((1, D), lambda s, c: (0, 0))
    state = lambda s, c: (0, s, 0, 0, 0)
    in_specs = [
        pl.BlockSpec((R, D), rows(0)), pl.BlockSpec((R, D), rows(1)), pl.BlockSpec((R, D), rows(2)),
        pl.BlockSpec((R, D), rows(0)), pl.BlockSpec((R, D), rows(0)), pl.BlockSpec((R, D), rows(0)),
        vec, vec, vec, vec, vec,
        pl.BlockSpec((D, H), lambda s, c: (0, 0)), pl.BlockSpec((H, D), lambda s, c: (0, 0)),
        pl.BlockSpec((None, nb, H, N, N), state),
    ]
    out_specs = [
        pl.BlockSpec((R, D), lambda s, c: (s * nc + c, 0)),
        pl.BlockSpec((None, nb, H, N, N), state),
    ]
    out_shape = [
        jax.ShapeDtypeStruct((n_seq * nc * L, D), BF16),
        jax.ShapeDtypeStruct((1, n_seq, H, N, N), F32),
    ]
    return pl.pallas_call(
        functools.partial(_rwkv_kernel, L=L, nb=nb, nc=nc),
        grid=(n_seq // nb, nc),
        in_specs=in_specs,
        out_specs=out_specs,
        out_shape=out_shape,
        scratch_shapes=[pltpu.VMEM((nb, RWKV_PAIRS, LANES, LANES), F32), pltpu.VMEM((R, D), F32)],
        compiler_params=_params("parallel", "arbitrary"),
        name="rwkv7_chunks",
    )(rkv, rkv, rkv, lw, aa, gg, k_k.reshape(1, D), k_a.reshape(1, D), r_k.reshape(1, D),
      ln_w.reshape(1, D), ln_b.reshape(1, D), onehot, onehot.T, s0)


def _mlstm_gate_epilogue(acc, bias):
    z = GATE_SOFTCAP * jnp.tanh((acc + bias) / GATE_SOFTCAP)
    lane = lax.broadcasted_iota(jnp.int32, z.shape, 1)
    return jnp.where(lane < MLSTM_HEADS, z, jax.nn.log_sigmoid(z))


def _ffn_ple(h, hn, p_pair, layer, w_gate, w_up, w_down, g_ple, w_ple_gate, w_ple_proj, next_gain, *, split_out):
    d_ff, d = w_down.shape[-2:]
    act = _proj(hn, [w_gate, w_up], out_dtype=BF16, tm=1024, tn=512, layer=layer,
                epilogue=lambda g, u: jax.nn.silu(g) * u, name="swiglu_up")
    h, hn = _res(act, w_down, h, g_ple, tm=256, tk=d_ff, hn_dtype=BF16, layer=layer, name="swiglu_down")
    return _res(hn, w_ple_gate, h, next_gain, tm=512, tk=d, hn_dtype=F32, x2=p_pair, w2=w_ple_proj,
                emit_h=split_out is None, split_out=split_out, layer=layer, name="ple")


def kernel(x_prompt, x_sample, state_mlstm_C, state_mlstm_n, state_mlstm_m, state_rwkv_S, state_rwkv_shift,
           p_prompt, p_sample, norm_mix, norm_ffn, norm_ple, norm_final, ffn_w_gate, ffn_w_up, ffn_w_down,
           ple_w_proj, ple_w_gate, mlstm_w_q, mlstm_w_k, mlstm_w_v, mlstm_w_igate, mlstm_b_igate,
           mlstm_w_fgate, mlstm_b_fgate, mlstm_w_ogate, mlstm_norm_w, mlstm_w_out, rwkv_mu, rwkv_w_r,
           rwkv_w_k, rwkv_w_v, rwkv_w_o, rwkv_w0, rwkv_w1, rwkv_w2, rwkv_a0, rwkv_a1, rwkv_a2, rwkv_g1,
           rwkv_g2, rwkv_k_k, rwkv_k_a, rwkv_r_k, rwkv_ln_w, rwkv_ln_b):
    D = D_MODEL
    B, T, _ = x_prompt.shape
    BS, TS, _ = x_sample.shape
    MP, MS = B * T, BS * TS
    M = MP + MS
    H, DK, DV = MLSTM_HEADS, MLSTM_DQK, MLSTM_DV
    RH, RN = RWKV_HEADS, RWKV_HEAD_DIM
    PD = p_prompt.shape[-1]

    x_pair = (x_prompt.reshape(MP, D), x_sample.reshape(MS, D))
    p_pairs = [(p_prompt[i].reshape(MP, PD), p_sample[i].reshape(MS, PD)) for i in range(2)]

    hn = _norm(x_pair, norm_mix[0], BF16)
    w_qkvo = jnp.concatenate([mlstm_w_q[0], mlstm_w_k[0], mlstm_w_v[0], mlstm_w_ogate[0]], axis=1).astype(BF16)
    col_scale = jnp.concatenate([jnp.ones((H * DK,), F32), jnp.full((H * DK,), DK ** -0.5, F32),
                                 jnp.ones((2 * H * DV,), F32)])
    qkvo = _proj(hn, [w_qkvo], out_dtype=BF16, tm=1024, tn=1024, extras=[col_scale],
                 epilogue=lambda a, s: a * s, name="mlstm_qkvo")
    w_gates = jnp.zeros((D, LANES), F32).at[:, :H].set(mlstm_w_igate[0]).at[:, H:2 * H].set(mlstm_w_fgate[0])
    b_gates = jnp.zeros((LANES,), F32).at[:H].set(mlstm_b_igate[0]).at[H:2 * H].set(mlstm_b_fgate[0])
    gates = _proj(hn, [w_gates.astype(BF16)], out_dtype=F32, tm=1024, tn=LANES, extras=[b_gates],
                  epilogue=_mlstm_gate_epilogue, name="mlstm_gates")

    LP = math.gcd(T, MLSTM_CHUNK)
    LS = math.gcd(TS, MLSTM_CHUNK)
    gp = gates.reshape(M // LP, LP, LANES)
    gs = gates[MP:].reshape(MS // LS, LS, LANES)
    hm_p, C_p, n_p, m_p = _mlstm(qkvo.reshape(M // LP, LP, -1), gp, gp.transpose(0, 2, 1), mlstm_norm_w[0],
                                 jnp.zeros((1, B, H, DK, DV), F32), jnp.zeros((B, H, DK), F32),
                                 jnp.zeros((B, H), F32), n_seq=B, nc=T // LP, L=LP, row0=0, out_dtype=BF16)
    hm_s, C_s, n_s, m_s = _mlstm(qkvo[MP:].astype(F32).reshape(MS // LS, LS, -1), gs, gs.transpose(0, 2, 1),
                                 mlstm_norm_w[0], state_mlstm_C, state_mlstm_n[0], state_mlstm_m[0],
                                 n_seq=BS, nc=TS // LS, L=LS, row0=0, out_dtype=F32)
    h, hn = _res((hm_p.reshape(MP, D), hm_s.reshape(MS, D)), mlstm_w_out[0].astype(BF16), x_pair, norm_ffn[0],
                 tm=512, tk=D, hn_dtype=BF16, name="mlstm_out")
    ffn_w = (ffn_w_gate, ffn_w_up, ffn_w_down.astype(BF16))
    ple_w = (ple_w_gate.astype(BF16), ple_w_proj.astype(BF16))
    h, xn = _ffn_ple(h, hn, p_pairs[0], 0, *ffn_w, norm_ple[0], *ple_w, norm_mix[1], split_out=None)

    tm_r = _row_tile(math.gcd(MP, MS), 512)
    shift = dict(tm=tm_r, t_prompt=T, t_sample=TS, n_prompt_tiles=MP // tm_r)
    sx = jnp.repeat(state_rwkv_shift[0], TS, axis=0)
    mu = rwkv_mu[0]
    w_rkv = jnp.concatenate([rwkv_w_r[0], rwkv_w_k[0], rwkv_w_v[0]], axis=1).astype(BF16)
    rkv = _rkv(xn, sx, jnp.stack([mu[0], mu[2], mu[3]]), w_rkv, tm=tm_r, tn=1024, shift=shift)

    rank =max(-(-w.shape[1] // LANES) * LANES for w in (rwkv_w1[0], rwkv_a1[0], rwkv_g1[0]))

    def pad_to(w1, w2):
        r = w1.shape[1]
        return (jnp.pad(w1, ((0, 0), (0, rank - r))).astype(BF16), jnp.pad(w2, ((0, rank - r), (0, 0))).astype(BF16))

    lora_w = [pad_to(rwkv_w1[0], rwkv_w2[0]), pad_to(rwkv_a1[0], rwkv_a2[0]), pad_to(rwkv_g1[0], rwkv_g2[0])]
    lw, aa, gg, last_rows = _lora(xn, sx, jnp.stack([mu[1], mu[4], mu[5]]), jnp.stack([w[0] for w in lora_w]),
                                  jnp.stack([w[1] for w in lora_w]),
                                  jnp.stack([rwkv_w0[0], rwkv_a0[0], jnp.zeros((D,), F32)]), tm=tm_r, shift=shift)
    shift_p = last_rows[T // TS - 1:MP // TS:T // TS]
    shift_s = last_rows[MP // TS:]

    LRP = math.gcd(T, RWKV_CHUNK)
    LRS = math.gcd(TS, RWKV_CHUNK)
    nbs = RWKV_CHUNK // LRS

    def run_rwkv(L, n_seq, nb, t_len, row0, s0):
        return _rwkv(rkv, lw, aa, gg, rwkv_k_k[0], rwkv_k_a[0], rwkv_r_k[0], rwkv_ln_w[0], rwkv_ln_b[0], s0,
                     n_seq=n_seq, nb=nb, nc=t_len // L, L=L, row0=row0)

    y_p, S_p = run_rwkv(LRP, B, 1, T, 0, jnp.zeros((1, B, RH, RN, RN), F32))
    y_s, S_s = run_rwkv(LRS, BS, nbs, TS, MP // (nbs * LRS), state_rwkv_S)
    h, hn = _res((y_p, y_s), rwkv_w_o[0].astype(BF16), h, norm_ffn[1], tm=512, tk=D, hn_dtype=BF16,
                 name="rwkv_out")
    y_prompt, y_sample = _ffn_ple(h, hn, p_pairs[1], 1, *ffn_w, norm_ple[1], *ple_w, norm_final, split_out=MP)

    return (y_prompt.reshape(B, T, D), y_sample.reshape(BS, TS, D),
            C_p, n_p[None], m_p.reshape(1, B, H), S_p, shift_p[None],
            C_s, n_s[None], m_s.reshape(1, BS, H), S_s, shift_s[None])
```

```python
import functools
import math

import jax
import jax.numpy as jnp
from jax import lax
from jax.experimental import pallas as pl
from jax.experimental.pallas import tpu as pltpu

F32 = jnp.float32
BF16 = jnp.bfloat16

D_MODEL = 2048
MLSTM_HEADS = 8
MLSTM_DQK = 128
MLSTM_DV = 256
MLSTM_CHUNK = 256
GATE_SOFTCAP = 15.0
RWKV_HEAD_DIM = 64
RWKV_HEADS = 32
RWKV_PAIRS = RWKV_HEADS // 2
RWKV_CHUNK = 64
NORM_EPS = 1e-6
RWKV_GN_EPS = 64e-5
L2_EPS = 1e-12
LANES = 128
SUBLANES = 8
VMEM_LIMIT_BYTES = 56 * 1024 * 1024


def _params(*sem):
    return pltpu.CompilerParams(dimension_semantics=sem, vmem_limit_bytes=VMEM_LIMIT_BYTES)


def _row_tile(m, preferred):
    t = preferred
    while m % t:
        t -= LANES
    return t


def _rms(x, g):
    return x * lax.rsqrt(jnp.mean(x * x, axis=-1, keepdims=True) + NORM_EPS) * g


def _dot(a, b):
    return jnp.dot(a, b, preferred_element_type=F32)


def _dot_nt(a, b):
    return lax.dot_general(a, b, (((1,), (1,)), ((), ())), preferred_element_type=F32)


def _dot_tn(a, b):
    return lax.dot_general(a, b, (((0,), (0,)), ((), ())), preferred_element_type=F32)


def _split3(x):
    hi = x.astype(BF16)
    r1 = x - hi.astype(F32)
    mid = r1.astype(BF16)
    lo = (r1 - mid.astype(F32)).astype(BF16)
    return hi, mid, lo


def _dot_sel(x, sel):
    hi, mid, lo = _split3(x)
    return _dot(hi, sel) + _dot(mid, sel) + _dot(lo, sel)


def _sel_dot(sel, x):
    hi, mid, lo = _split3(x)
    return _dot(sel, hi) + _dot(sel, mid) + _dot(sel, lo)


def _sel_dot_nt(x, sel):
    hi, mid, lo = _split3(x)
    return _dot_nt(hi, sel) + _dot_nt(mid, sel) + _dot_nt(lo, sel)


def _row_specs(x, tm, cols, col_index):
    if isinstance(x, tuple):
        a, b = x
        na = a.shape[0] // tm
        return ([pl.BlockSpec((tm, cols), lambda i, *r: (jnp.minimum(i, na - 1), col_index(*r))),
                 pl.BlockSpec((tm, cols), lambda i, *r: (jnp.maximum(i - na, 0), col_index(*r)))], [a, b], na)
    return [pl.BlockSpec((tm, cols), lambda i, *r: (i, col_index(*r)))], [x], None


def _rows(x):
    return sum(a.shape[0] for a in x) if isinstance(x, tuple) else x.shape[0]


def _pair_row_tile(operands, preferred, extra_rows=()):
    halves = [a.shape[0] for x in operands if x is not None for a in (x if isinstance(x, tuple) else (x,))]
    return _row_tile(math.gcd(*halves, *extra_rows), preferred)


def _load_rows(refs, na, dtype=None):
    vals = [r[...] if dtype is None else r[...].astype(dtype) for r in refs]
    if na is None:
        return vals[0]
    return jnp.where(pl.program_id(0) < na, vals[0], vals[1])


def _store_rows(refs, na, val):
    if na is None:
        refs[0][...] = val.astype(refs[0].dtype)
        return

    @pl.when(pl.program_id(0) < na)
    def _():
        refs[0][...] = val.astype(refs[0].dtype)

    @pl.when(pl.program_id(0) >= na)
    def _():
        refs[1][...] = val.astype(refs[1].dtype)


_zero_col = lambda *r: 0


def _norm_kernel(*refs, na):
    nx = 1 if na is None else 2
    g_ref, o_ref = refs[nx], refs[nx + 1]
    o_ref[...] = _rms(_load_rows(refs[:nx], na), g_ref[...]).astype(o_ref.dtype)


def _norm(x, g, out_dtype, tm=512):
    m = _rows(x)
    d = g.shape[0]
    tm = _pair_row_tile([x], tm)
    x_specs, x_args, na = _row_specs(x, tm, d, _zero_col)
    return pl.pallas_call(
        functools.partial(_norm_kernel, na=na),
        grid=(m // tm,),
        in_specs=x_specs + [pl.BlockSpec((1, d), lambda i: (0, 0))],
        out_specs=pl.BlockSpec((tm, d), lambda i: (i, 0)),
        out_shape=jax.ShapeDtypeStruct((m, d), out_dtype),
        compiler_params=_params("parallel"),
        name="rmsnorm",
    )(*x_args, g.reshape(1, d))


def _proj_kernel(*refs, nw, ne, epilogue):
    x_ref = refs[0]
    ws = refs[1:1 + nw]
    es = refs[1 + nw:1 + nw + ne]
    o_ref = refs[1 + nw + ne]
    xb = x_ref[...]
    accs = [_dot(xb, w[...].astype(BF16)) for w in ws]
    o_ref[...] = epilogue(*accs, *[e[...] for e in es]).astype(o_ref.dtype)


def _proj(x, ws, *, out_dtype, tm, tn, extras=(), epilogue=lambda a: a, layer=None, name="proj"):
    m, k = x.shape
    n = ws[0].shape[-1]
    tm = _row_tile(m, tm)
    if layer is None:
        w_spec = pl.BlockSpec((k, tn), lambda i, j: (0, j))
    else:
        w_spec = pl.BlockSpec((None, k, tn), lambda i, j: (layer, 0, j))
    in_specs = ([pl.BlockSpec((tm, k), lambda i, j: (i, 0))]
                + [w_spec for _ in ws]
                + [pl.BlockSpec((1, tn), lambda i, j: (0, j)) for _ in extras])
    return pl.pallas_call(
        functools.partial(_proj_kernel, nw=len(ws), ne=len(extras), epilogue=epilogue),
        grid=(m // tm, n // tn),
        in_specs=in_specs,
        out_specs=pl.BlockSpec((tm, tn), lambda i, j: (i, j)),
        out_shape=jax.ShapeDtypeStruct((m, n), out_dtype),
        compiler_params=_params("parallel", "arbitrary"),
        name=name,
    )(x, *ws, *[e.reshape(1, n) for e in extras])


def _res_kernel(*refs, nk, gated, emit_h, na_x, na_res, na_x2, na_out):
    pos = [0]

    def take(n):
        out = refs[pos[0]:pos[0] + n]
        pos[0] += n
        return out

    width = lambda na: 1 if na is None else 2
    x_refs = take(width(na_x))
    (w_ref,) = take(1)
    res_refs = take(width(na_res))
    (g_ref,) = take(1)
    if gated:
        x2_refs = take(width(na_x2))
        (w2_ref,) = take(1)
    h_refs = take(1) if emit_h else ()
    hn_refs = take(width(na_out))
    acc_refs = take(1) if nk > 1 else ()
    k = pl.program_id(1)

    part = _dot(_load_rows(x_refs, na_x, BF16), w_ref[...])
    if nk > 1:
        acc_ref = acc_refs[0]

        @pl.when(k == 0)
        def _():
            acc_ref[...] = part

        @pl.when(k > 0)
        def _():
            acc_ref[...] += part

    @pl.when(k == nk - 1)
    def _():
        a = acc_refs[0][...] if nk > 1 else part
        res = _load_rows(res_refs, na_res)
        if gated:
            h = res + jax.nn.sigmoid(a) * _dot(_load_rows(x2_refs, na_x2, BF16), w2_ref[...])
        else:
            h = res + a
        if emit_h:
            h_refs[0][...] = h
        _store_rows(hn_refs, na_out, _rms(h, g_ref[...]))


def _res(x, w, res, gain, *, tm, tk, hn_dtype, x2=None, w2=None, emit_h=True, split_out=None, layer=None,
         name="res"):
    m = _rows(x)
    kdim, d = w.shape[-2:]
    lead = () if layer is None else (None,)
    at_layer = (lambda *idx: idx) if layer is None else (lambda *idx: (layer,) + idx)
    tm = _pair_row_tile([x, res, x2], tm, () if split_out is None else (split_out, m - split_out))
    nk = kdim // tk
    gated = x2 is not None
    x_specs, x_args, na_x = _row_specs(x, tm, tk, lambda k: k)
    res_specs, res_args, na_res = _row_specs(res, tm, d, _zero_col)
    const = lambda i, k: (0, 0)
    w_mode = dict(pipeline_mode=pl.Buffered(1)) if nk == 1 else {}
    in_specs = (x_specs + [pl.BlockSpec(lead + (tk, d), lambda i, k: at_layer(k, 0), **w_mode)]
                + res_specs + [pl.BlockSpec((1, d), const)])
    args = x_args + [w] + res_args + [gain.reshape(1, d)]
    na_x2 = None
    if gated:
        k2 = w2.shape[-2]
        x2_specs, x2_args, na_x2 = _row_specs(x2, tm, k2, _zero_col)
        in_specs += x2_specs + [pl.BlockSpec(lead + (k2, d), lambda i, k: at_layer(0, 0),
                                             pipeline_mode=pl.Buffered(1))]
        args += x2_args + [w2]
    out_specs, out_shape = [], []
    if emit_h:
        out_specs.append(pl.BlockSpec((tm, d), lambda i, k: (i, 0)))
        out_shape.append(jax.ShapeDtypeStruct((m, d), F32))
    na_out = None
    if split_out is None:
        out_specs.append(pl.BlockSpec((tm, d), lambda i, k: (i, 0)))
        out_shape.append(jax.ShapeDtypeStruct((m, d), hn_dtype))
    else:
        na_out = split_out // tm
        out_specs += [pl.BlockSpec((tm, d), lambda i, k: (jnp.minimum(i, na_out - 1), 0)),
                      pl.BlockSpec((tm, d), lambda i, k: (jnp.maximum(i - na_out, 0), 0))]
        out_shape += [jax.ShapeDtypeStruct((split_out, d), hn_dtype),
                      jax.ShapeDtypeStruct((m - split_out, d), hn_dtype)]
    return pl.pallas_call(
        functools.partial(_res_kernel, nk=nk, gated=gated, emit_h=emit_h, na_x=na_x, na_res=na_res,
                          na_x2=na_x2, na_out=na_out),
        grid=(m // tm, nk),
        in_specs=in_specs,
        out_specs=out_specs,
        out_shape=out_shape,
        scratch_shapes=[pltpu.VMEM((tm, d), F32)] if nk > 1 else [],
        compiler_params=_params("parallel", "arbitrary"),
        name=name,
    )(*args)


def _mlstm_kernel(q_ref, k_ref, v_ref, og_ref, g_ref, gt_ref, nw_ref, c0_ref, n0_ref, m0_ref,
                  h_ref, c_ref, n_ref, m_ref, *, L):
    H, DK, DV = MLSTM_HEADS, MLSTM_DQK, MLSTM_DV

    @pl.when(pl.program_id(1) == 0)
    def _():
        c_ref[...] = c0_ref[...]
        n_ref[...] = n0_ref[...]
        m_ref[...] = m0_ref[...]

    q = q_ref[...].astype(BF16)
    k = k_ref[...].astype(BF16)
    v = v_ref[...].astype(BF16)
    og = og_ref[...]
    G = g_ref[...]
    GT = gt_ref[...]
    nw = nw_ref[...]
    m_prev = m_ref[0]

    row = lax.broadcasted_iota(jnp.int32, (L, L), 0)
    col = lax.broadcasted_iota(jnp.int32, (L, L), 1)
    causal = col <= row
    tri = causal.astype(BF16)
    b_col = _sel_dot(tri, G)
    b_row = _sel_dot_nt(GT, tri)
    lane_h = lax.broadcasted_iota(jnp.int32, (1, H), 1)

    hs = range(H)
    qh = [q[:, h * DK:(h + 1) * DK] for h in hs]
    kh = [k[:, h * DK:(h + 1) * DK] for h in hs]
    vh = [v[:, h * DV:(h + 1) * DV] for h in hs]
    ch = [c_ref[0, h] for h in hs]
    nh = [n_ref[0, h:h + 1, :] for h in hs]
    bc = [b_col[:, H + h:H + h + 1] for h in hs]
    li_c = [G[:, h:h + 1] for h in hs]
    m_h = [m_prev[:, h:h + 1] for h in hs]
    qk = [_dot_nt(qh[h], kh[h]) for h in hs]
    qc = [_dot(qh[h], ch[h].astype(BF16)) for h in hs]
    dlog = [jnp.where(causal, bc[h] - b_row[H + h:H + h + 1, :] + GT[h:h + 1, :], -jnp.inf) for h in hs]
    a = [bc[h] + m_h[h] for h in hs]
    m_t = [jnp.maximum(a[h], jnp.max(dlog[h], axis=1, keepdims=True)) for h in hs]
    w_inter = [jnp.exp(a[h] - m_t[h]) for h in hs]
    s = [qk[h] * jnp.exp(dlog[h] - m_t[h]) for h in hs]
    sv = [_dot(s[h].astype(BF16), vh[h]) for h in hs]
    m_new = [m_t[h][L - 1:L, :] for h in hs]
    b_last = [bc[h][L - 1:L, :] for h in hs]
    wk = [jnp.exp(b_last[h] - bc[h] + li_c[h] - m_new[h]) for h in hs]
    kv = [_dot_tn(kh[h], (vh[h].astype(F32) * wk[h]).astype(BF16)) for h in hs]
    decay = [jnp.exp(b_last[h] + m_h[h] - m_new[h]) for h in hs]
    for h in hs:
        c_ref[0, h] = decay[h] * ch[h] + kv[h]
        n_ref[0, h:h + 1, :] = decay[h] * nh[h] + jnp.sum(kh[h].astype(F32) * wk[h], axis=0, keepdims=True)
    m_out = jnp.zeros((1, H), F32)
    for h in hs:
        m_out = jnp.where(lane_h == h, m_new[h], m_out)
    m_ref[0] = m_out

    for h in hs:
        num = w_inter[h] * qc[h] + sv[h]
        den = (w_inter[h] * jnp.sum(qh[h].astype(F32) * nh[h], axis=1, keepdims=True)
               + jnp.sum(s[h], axis=1, keepdims=True))
        hh = num / jnp.maximum(jnp.abs(den), jnp.exp(-m_t[h]))
        hh = hh * lax.rsqrt(jnp.mean(hh * hh, axis=1, keepdims=True) + NORM_EPS)
        o = jax.nn.sigmoid(og[:, h * DV:(h + 1) * DV].astype(F32))
        h_ref[:, h * DV:(h + 1) * DV] =(o * (hh * nw[:, h * DV:(h + 1) * DV])).astype(h_ref.dtype)


def _mlstm(qkvo, g3, gt3, nw, c0, n0, m0, *, n_seq, nc, L, row0, out_dtype):
    H, DK, DV = MLSTM_HEADS, MLSTM_DQK, MLSTM_DV
    nqk, nv = H * DK, H * DV
    flat = qkvo.ndim == 2

    def rows(colblk):
        return lambda s, c: (row0 + s * nc + c, 0, colblk)

    def tok(width, colblk):
        if flat:
            return pl.BlockSpec((L, width), lambda s, c: (row0 + s * nc + c, colblk))
        return pl.BlockSpec((None, L, width), rows(colblk))

    state = lambda s, c: (0, s, 0, 0, 0)
    state3 = lambda s, c: (s, 0, 0)
    in_specs = [
        tok(nqk, 0),
        tok(nqk, 1),
        tok(nv, 1),
        tok(nv, 2),
        pl.BlockSpec((None, L, LANES), rows(0)),
        pl.BlockSpec((None, LANES, L), rows(0)),
        pl.BlockSpec((1, nv), lambda s, c: (0, 0)),
        pl.BlockSpec((None, 1, H, DK, DV), state),
        pl.BlockSpec((1, H, DK), state3),
        pl.BlockSpec((1, 1, H), state3),
    ]
    out_specs = [
        (pl.BlockSpec((L, nv), lambda s, c: (s * nc + c, 0)) if flat
         else pl.BlockSpec((None, L, nv), lambda s, c: (s * nc + c, 0, 0))),
        pl.BlockSpec((None, 1, H, DK, DV), state),
        pl.BlockSpec((1, H, DK), state3),
        pl.BlockSpec((1, 1, H), state3),
    ]
    out_shape = [
        jax.ShapeDtypeStruct((n_seq * nc * L, nv) if flat else (n_seq * nc, L, nv), out_dtype),
        jax.ShapeDtypeStruct((1, n_seq, H, DK, DV), F32),
        jax.ShapeDtypeStruct((n_seq, H, DK), F32),
        jax.ShapeDtypeStruct((n_seq, 1, H), F32),
    ]
    return pl.pallas_call(
        functools.partial(_mlstm_kernel, L=L),
        grid=(n_seq, nc),
        in_specs=in_specs,
        out_specs=out_specs,
        out_shape=out_shape,
        compiler_params=_params("parallel", "arbitrary"),
        name="mlstm_chunks",
    )(qkvo, qkvo, qkvo, qkvo, g3, gt3, nw.reshape(1, nv), c0, n0, m0.reshape(n_seq, 1, H))


def _shifted(x_ref, prev_ref, sx_ref, *, tm, t_prompt, t_sample, n_prompt_tiles):
    i = pl.program_id(0)
    x = x_ref[...]
    rolled = pltpu.roll(x, 1, axis=0)
    rowi = lax.broadcasted_iota(jnp.int32, (tm, 1), 0)
    first = jnp.where((i * tm) % t_prompt == 0, 0.0, prev_ref[SUBLANES - 1:SUBLANES, :])
    xprev = jnp.where(rowi == 0, first, rolled)
    at_sample_start = jnp.logical_and(i >= n_prompt_tiles, rowi % t_sample == 0)
    return x, jnp.where(at_sample_start, sx_ref[...], xprev)


def _shift_specs(tm, d, n_prompt_tiles):
    return [pl.BlockSpec((tm, d), lambda i, *r: (i, 0)),
            pl.BlockSpec((SUBLANES, d), lambda i, *r: (jnp.maximum(i * (tm // SUBLANES) - 1, 0), 0)),
            pl.BlockSpec((tm, d), lambda i, *r: (jnp.maximum(i - n_prompt_tiles, 0), 0))]


def _rkv_kernel(x_ref, prev_ref, sx_ref, mu_ref, w_ref, o_ref, xb_ref, *, tiles_per_group, shift):
    j = pl.program_id(1)

    @pl.when(j == 0)
    def _():
        x, xp = _shifted(x_ref, prev_ref, sx_ref, **shift)
        dx = xp - x
        for c in range(3):
            xb_ref[c] = (x + dx * mu_ref[c:c + 1, :]).astype(BF16)

    o_ref[...] = _dot(xb_ref[j // tiles_per_group], w_ref[...]).astype(o_ref.dtype)


def _rkv(xn, sx, mu3, w_rkv, *, tm, tn, shift):
    m, d = xn.shape
    n = w_rkv.shape[1]
    return pl.pallas_call(
        functools.partial(_rkv_kernel, tiles_per_group=d // tn, shift=shift),
        grid=(m // tm, n // tn),
        in_specs=_shift_specs(tm, d, shift["n_prompt_tiles"]) + [
            pl.BlockSpec((3, d), lambda i, j: (0, 0)), pl.BlockSpec((d, tn), lambda i, j: (0, j))],
        out_specs=pl.BlockSpec((tm, tn), lambda i, j: (i, j)),
        out_shape=jax.ShapeDtypeStruct((m, n), BF16),
        scratch_shapes=[pltpu.VMEM((3, tm, d), BF16)],
        compiler_params=_params("parallel", "arbitrary"),
        name="rwkv_rkv",
    )(xn, xn, sx, mu3, w_rkv)


def _lora_kernel(x_ref, prev_ref, sx_ref, mu_ref, w1_ref, w2_ref, b_ref, lw_ref, a_ref, g_ref, last_ref, *,
                 shift):
    x, xp = _shifted(x_ref, prev_ref, sx_ref, **shift)
    dx = xp - x
    n_last, tm = last_ref.shape[0], x.shape[0]
    pick = (lax.broadcasted_iota(jnp.int32, (n_last, tm), 1)
            == lax.broadcasted_iota(jnp.int32, (n_last, tm), 0) * (tm // n_last) + (tm // n_last - 1))
    last_ref[...] = _sel_dot(pick.astype(BF16), x)

    def branch(c, mid):
        xm = (x + dx * mu_ref[c:c + 1, :]).astype(BF16)
        hid = mid(_dot(xm, w1_ref[c])).astype(BF16)
        return b_ref[c:c + 1, :] + _dot(hid, w2_ref[c])

    lw_ref[...] = -math.exp(-0.5) * jax.nn.sigmoid(branch(0, jnp.tanh))
    a_ref[...] = jax.nn.sigmoid(branch(1, lambda z: z))
    g_ref[...] = branch(2, jax.nn.sigmoid).astype(g_ref.dtype)


def _lora(xn, sx, mu3, w1s, w2s, bias3, *, tm, shift):
    m, d = xn.shape
    r = w1s.shape[2]
    row = pl.BlockSpec((tm, d), lambda i: (i, 0))
    return pl.pallas_call(
        functools.partial(_lora_kernel, shift=shift),
        grid=(m // tm,),
        in_specs=_shift_specs(tm, d, shift["n_prompt_tiles"]) + [
            pl.BlockSpec((3, d), lambda i: (0, 0)), pl.BlockSpec((3, d, r), lambda i: (0, 0, 0)),
            pl.BlockSpec((3, r, d), lambda i: (0, 0, 0)), pl.BlockSpec((3, d), lambda i: (0, 0))],
        out_specs=[row, row, row, pl.BlockSpec((tm // shift["t_sample"], d), lambda i: (i, 0))],
        out_shape=[jax.ShapeDtypeStruct((m, d), F32), jax.ShapeDtypeStruct((m, d), F32),
                   jax.ShapeDtypeStruct((m, d), BF16), jax.ShapeDtypeStruct((m // shift["t_sample"], d), F32)],
        compiler_params=_params("parallel"),
        name="rwkv_lora",
    )(xn, xn, sx, mu3, w1s, w2s, bias3)


def _rwkv_kernel(r_ref, k_ref, v_ref, lw_ref, a_ref, g_ref, kk_ref, ka_ref, rk_ref, lnw_ref, lnb_ref,
                 e_ref, et_ref, s0_ref, o_ref, s_ref, sbd_ref, y_ref, *, L, nb, nc):
    N = RWKV_HEAD_DIM
    R = nb * L
    R2 = 2 * R
    c = pl.program_id(1)
    lane = lax.broadcasted_iota(jnp.int32, (1, LANES), 1)
    head0 = lane < N
    seqs = range(nb)
    pairs = range(RWKV_PAIRS)

    @pl.when(c == 0)
    def _():
        z = jnp.zeros((N, N), F32)
        for j in seqs:
            for p in pairs:
                top = jnp.concatenate([s0_ref[j, 2 * p], z], axis=1)
                bot = jnp.concatenate([z, s0_ref[j, 2 * p + 1]], axis=1)
                sbd_ref[j, p] = jnp.concatenate([top, bot], axis=0)

    E = e_ref[...]
    ET = et_ref[...]

    def seg_sum_bcast(x):
        return _dot_sel(_dot_sel(x, E), ET)

    r = r_ref[...].astype(F32)
    k = k_ref[...].astype(F32)
    v = v_ref[...].astype(F32)
    lw = lw_ref[...]
    a = a_ref[...]
    kk = k * kk_ref[...]
    kk = kk / jnp.maximum(jnp.sqrt(seg_sum_bcast(kk * kk)), L2_EPS)
    k2 = k * (1.0 + (a - 1.0) * ka_ref[...])

    row = lax.broadcasted_iota(jnp.int32, (R, R), 0)
    col = lax.broadcasted_iota(jnp.int32, (R, R), 1)
    tri = ((col <= row) & (row // L == col // L)).astype(BF16)
    logp = _sel_dot(tri, lw)
    p_incl = jnp.exp(logp)
    inv_p = jnp.exp(-logp)
    at = -kk * jnp.exp(logp - lw)
    bt = kk * a * inv_p
    kt = k2 * inv_p
    rt = r * p_incl

    row2 = lax.broadcasted_iota(jnp.int32, (R2, R2), 0)
    col2 = lax.broadcasted_iota(jnp.int32, (R2, R2), 1)
    same = row2 // L == col2 // L
    strict = same & (col2 < row2)
    incl = same & (col2 <= row2)
    eye = (row2 == col2).astype(F32)
    n_sq = max(int(math.ceil(math.log2(L))) - 1, 0)

    def stack(x):
        parts = []
        for j in seqs:
            xj = x[j * L:(j + 1) * L]
            parts += [jnp.where(head0, xj, 0.0), jnp.where(head0, 0.0, xj)]
        return jnp.concatenate(parts, axis=0)

    sls = [slice(p * LANES, (p + 1) * LANES) for p in pairs]
    seq_rows = [slice(j * 2 * L, (j + 1) * 2 * L) for j in seqs]
    xa = [stack(at[:, s]).astype(BF16) for s in sls]
    xr = [stack(rt[:, s]).astype(BF16) for s in sls]
    yb = [stack(bt[:, s]).astype(BF16) for s in sls]
    yk = [stack(kt[:, s]).astype(BF16) for s in sls]
    vr = [stack(v[:, s]).astype(BF16) for s in sls]
    sbd = [[sbd_ref[j, p] for p in pairs] for j in seqs]
    sb = [[x.astype(BF16) for x in sj] for sj in sbd]

    big = [_dot_nt(jnp.concatenate([xa[p], xr[p]], axis=0), jnp.concatenate([yb[p], yk[p]], axis=0))
           for p in pairs]
    n_ab = [jnp.where(strict, big[p][:R2, :R2], 0.0) for p in pairs]
    a_akrk = [jnp.concatenate([jnp.where(strict, big[p][:R2, R2:], 0.0), jnp.where(incl, big[p][R2:, R2:], 0.0)],
                              axis=0).astype(BF16) for p in pairs]
    a_rb = [jnp.where(incl, big[p][R2:, :R2], 0.0).astype(BF16) for p in pairs]

    t_inv = [eye + n for n in n_ab]
    pw = [n.astype(BF16) for n in n_ab]
    for _ in range(n_sq):
        pw = [_dot(x, x).astype(BF16) for x in pw]
        t_inv = [t + _dot(t.astype(BF16), x) for t, x in zip(t_inv, pw)]

    xs = [[_dot_nt(jnp.concatenate([xa[p][seq_rows[j]], xr[p][seq_rows[j]]], axis=0), sb[j][p]) for j in seqs]
          for p in pairs]
    xsa = [jnp.concatenate([xs[p][j][:2 * L] for j in seqs], axis=0) for p in pairs]
    xsr = [jnp.concatenate([xs[p][j][2 * L:] for j in seqs], axis=0) for p in pairs]
    av = [_dot(a_akrk[p], vr[p]) for p in pairs]
    ub = [_dot(t_inv[p].astype(BF16), (xsa[p] + av[p][:R2]).astype(BF16)).astype(BF16) for p in pairs]
    yr = [xsr[p] + _dot(a_rb[p], ub[p]) + av[p][R2:] for p in pairs]
    for p in pairs:
        for j in seqs:
            lo = j * 2 * L
            y_ref[j * L:(j + 1) * L, sls[p]] = yr[p][lo:lo + L] + yr[p][lo + L:lo + 2 * L]
    for j in seqs:
        p_last = p_incl[(j + 1) * L - 1:(j + 1) * L, :]
        for p in pairs:
            upd = _dot_tn(jnp.concatenate([ub[p][seq_rows[j]], vr[p][seq_rows[j]]], axis=0),
                          jnp.concatenate([yb[p][seq_rows[j]], yk[p][seq_rows[j]]], axis=0))
            sbd_ref[j, p] = (sbd[j][p] + upd) * p_last[:, sls[p]]

    y = y_ref[...]
    mean = seg_sum_bcast(y) * (1.0 / N)
    yc = y - mean
    var = seg_sum_bcast(yc * yc) * (1.0 / N)
    yn = yc * lax.rsqrt(var + RWKV_GN_EPS) * lnw_ref[...] + lnb_ref[...]
    bonus = seg_sum_bcast(r * k2 * rk_ref[...]) * v
    o_ref[...] = ((yn + bonus) * g_ref[...].astype(F32)).astype(o_ref.dtype)

    @pl.when(c == nc - 1)
    def _():
        for j in seqs:
            for p in pairs:
                blk_p = sbd_ref[j, p]
                s_ref[j, 2 * p] = blk_p[:N, :N]
                s_ref[j, 2 * p + 1] = blk_p[N:, N:]


def _rwkv(rkv, lw, aa, gg, k_k, k_a, r_k, ln_w, ln_b, s0, *, n_seq, nb, nc, L, row0):
    D, H, N = D_MODEL, RWKV_HEADS, RWKV_HEAD_DIM
    R = nb * L
    assert nb == 1 or nc == 1
    onehot = (jnp.arange(D)[:, None] // N == jnp.arange(H)[None, :]).astype(BF16)

    def rows(colblk):
        return lambda s, c: (row0 + s * nc + c, colblk)

    vec = pl.BlockSpec((1, D), lambda s, c: (0, 0))
    state = lambda s, c: (0, s, 0, 0, 0)
    in_specs = [
        pl.BlockSpec((R, D), rows(0)), pl.BlockSpec((R, D), rows(1)), pl.BlockSpec((R, D), rows(2)),
        pl.BlockSpec((R, D), rows(0)), pl.BlockSpec((R, D), rows(0)), pl.BlockSpec((R, D), rows(0)),
        vec, vec, vec, vec, vec,
        pl.BlockSpec((D, H), lambda s, c: (0, 0)), pl.BlockSpec((H, D), lambda s, c: (0, 0)),
        pl.BlockSpec((None, nb, H, N, N), state),
    ]
    out_specs = [
        pl.BlockSpec((R, D), lambda s, c: (s * nc + c, 0)),
        pl.BlockSpec((None, nb, H, N, N), state),
    ]
    out_shape = [
        jax.ShapeDtypeStruct((n_seq * nc * L, D), BF16),
        jax.ShapeDtypeStruct((1, n_seq, H, N, N), F32),
    ]
    return pl.pallas_call(
        functools.partial(_rwkv_kernel, L=L, nb=nb, nc=nc),
        grid=(n_seq // nb, nc),
        in_specs=in_specs,
        out_specs=out_specs,
        out_shape=out_shape,
        scratch_shapes=[pltpu.VMEM((nb, RWKV_PAIRS, LANES, LANES), F32), pltpu.VMEM((R, D), F32)],
        compiler_params=_params("parallel", "arbitrary"),
        name="rwkv7_chunks",
    )(rkv, rkv, rkv, lw, aa, gg, k_k.reshape(1, D), k_a.reshape(1, D), r_k.reshape(1, D),
      ln_w.reshape(1, D), ln_b.reshape(1, D), onehot, onehot.T, s0)


def _mlstm_gate_epilogue(acc, bias):
    z = GATE_SOFTCAP * jnp.tanh((acc + bias) / GATE_SOFTCAP)
    lane = lax.broadcasted_iota(jnp.int32, z.shape, 1)
    return jnp.where(lane < MLSTM_HEADS, z, jax.nn.log_sigmoid(z))


def _ffn_ple(h, hn, p_pair, layer, w_gate, w_up, w_down, g_ple, w_ple_gate, w_ple_proj, next_gain, *, split_out):
    d_ff, d = w_down.shape[-2:]
    act = _proj(hn, [w_gate, w_up], out_dtype=BF16, tm=1024, tn=512, layer=layer,
                epilogue=lambda g, u: jax.nn.silu(g) * u, name="swiglu_up")
    h, hn = _res(act, w_down, h, g_ple, tm=256, tk=d_ff, hn_dtype=BF16, layer=layer, name="swiglu_down")
    return _res(hn, w_ple_gate, h, next_gain, tm=512, tk=d, hn_dtype=F32, x2=p_pair, w2=w_ple_proj,
                emit_h=split_out is None, split_out=split_out, layer=layer, name="ple")


def kernel(x_prompt, x_sample, state_mlstm_C, state_mlstm_n, state_mlstm_m, state_rwkv_S, state_rwkv_shift,
           p_prompt, p_sample, norm_mix, norm_ffn, norm_ple, norm_final, ffn_w_gate, ffn_w_up, ffn_w_down,
           ple_w_proj, ple_w_gate, mlstm_w_q, mlstm_w_k, mlstm_w_v, mlstm_w_igate, mlstm_b_igate,
           mlstm_w_fgate, mlstm_b_fgate, mlstm_w_ogate, mlstm_norm_w, mlstm_w_out, rwkv_mu, rwkv_w_r,
           rwkv_w_k, rwkv_w_v, rwkv_w_o, rwkv_w0, rwkv_w1, rwkv_w2, rwkv_a0, rwkv_a1, rwkv_a2, rwkv_g1,
           rwkv_g2, rwkv_k_k, rwkv_k_a, rwkv_r_k, rwkv_ln_w, rwkv_ln_b):
    D = D_MODEL
    B, T, _ = x_prompt.shape
    BS, TS, _ = x_sample.shape
    MP, MS = B * T, BS * TS
    M = MP + MS
    H, DK, DV = MLSTM_HEADS, MLSTM_DQK, MLSTM_DV
    RH, RN = RWKV_HEADS, RWKV_HEAD_DIM
    PD = p_prompt.shape[-1]

    x_pair = (x_prompt.reshape(MP, D), x_sample.reshape(MS, D))
    p_pairs = [(p_prompt[i].reshape(MP, PD), p_sample[i].reshape(MS, PD)) for i in range(2)]

    hn = _norm(x_pair, norm_mix[0], BF16)
    w_qkvo = jnp.concatenate([mlstm_w_q[0], mlstm_w_k[0], mlstm_w_v[0], mlstm_w_ogate[0]], axis=1).astype(BF16)
    col_scale = jnp.concatenate([jnp.ones((H * DK,), F32), jnp.full((H * DK,), DK ** -0.5, F32),
                                 jnp.ones((2 * H * DV,), F32)])
    qkvo = _proj(hn, [w_qkvo], out_dtype=BF16, tm=1024, tn=1024, extras=[col_scale],
                 epilogue=lambda a, s: a * s, name="mlstm_qkvo")
    w_gates = jnp.zeros((D, LANES), F32).at[:, :H].set(mlstm_w_igate[0]).at[:, H:2 * H].set(mlstm_w_fgate[0])
    b_gates = jnp.zeros((LANES,), F32).at[:H].set(mlstm_b_igate[0]).at[H:2 * H].set(mlstm_b_fgate[0])
    gates = _proj(hn, [w_gates.astype(BF16)], out_dtype=F32, tm=1024, tn=LANES, extras=[b_gates],
                  epilogue=_mlstm_gate_epilogue, name="mlstm_gates")

    LP = math.gcd(T, MLSTM_CHUNK)
    LS = math.gcd(TS, MLSTM_CHUNK)
    gp = gates.reshape(M // LP, LP, LANES)
    gs = gates[MP:].reshape(MS // LS, LS, LANES)
    hm_p, C_p, n_p, m_p = _mlstm(qkvo, gp, gp.transpose(0, 2, 1), mlstm_norm_w[0],
                                 jnp.zeros((1, B, H, DK, DV), F32), jnp.zeros((B, H, DK), F32),
                                 jnp.zeros((B, H), F32), n_seq=B, nc=T // LP, L=LP, row0=0, out_dtype=BF16)
    hm_s, C_s, n_s, m_s = _mlstm(qkvo[MP:].astype(F32).reshape(MS // LS, LS, -1), gs, gs.transpose(0, 2, 1),
                                 mlstm_norm_w[0], state_mlstm_C, state_mlstm_n[0], state_mlstm_m[0],
                                 n_seq=BS, nc=TS // LS, L=LS, row0=0, out_dtype=F32)
    h, hn = _res((hm_p, hm_s.reshape(MS, D)), mlstm_w_out[0].astype(BF16), x_pair, norm_ffn[0],
                 tm=512, tk=D, hn_dtype=BF16, name="mlstm_out")
    ffn_w = (ffn_w_gate, ffn_w_up, ffn_w_down.astype(BF16))
    ple_w = (ple_w_gate.astype(BF16), ple_w_proj.astype(BF16))
    h, xn = _ffn_ple(h, hn, p_pairs[0], 0, *ffn_w, norm_ple[0], *ple_w, norm_mix[1], split_out=None)

    tm_r = _row_tile(math.gcd(MP, MS), 512)
    shift = dict(tm=tm_r, t_prompt=T, t_sample=TS, n_prompt_tiles=MP // tm_r)
    sx = jnp.repeat(state_rwkv_shift[0], TS, axis=0)
    mu = rwkv_mu[0]
    w_rkv = jnp.concatenate([rwkv_w_r[0], rwkv_w_k[0], rwkv_w_v[0]], axis=1).astype(BF16)
    rkv = _rkv(xn, sx, jnp.stack([mu[0], mu[2], mu[3]]), w_rkv, tm=tm_r, tn=D, shift=shift)

    rank =max(-(-w.shape[1] // LANES) * LANES for w in (rwkv_w1[0], rwkv_a1[0], rwkv_g1[0]))

    def pad_to(w1, w2):
        r = w1.shape[1]
        return (jnp.pad(w1, ((0, 0), (0, rank - r))).astype(BF16), jnp.pad(w2, ((0, rank - r), (0, 0))).astype(BF16))

    lora_w = [pad_to(rwkv_w1[0], rwkv_w2[0]), pad_to(rwkv_a1[0], rwkv_a2[0]), pad_to(rwkv_g1[0], rwkv_g2[0])]
    lw, aa, gg, last_rows = _lora(xn, sx, jnp.stack([mu[1], mu[4], mu[5]]), jnp.stack([w[0] for w in lora_w]),
                                  jnp.stack([w[1] for w in lora_w]),
                                  jnp.stack([rwkv_w0[0], rwkv_a0[0], jnp.zeros((D,), F32)]), tm=tm_r, shift=shift)
    shift_p = last_rows[T // TS - 1:MP // TS:T // TS]
    shift_s = last_rows[MP // TS:]

    LRP = math.gcd(T, RWKV_CHUNK)
    LRS = math.gcd(TS, RWKV_CHUNK)
    nbs = RWKV_CHUNK // LRS

    def run_rwkv(L, n_seq, nb, t_len, row0, s0):
        return _rwkv(rkv, lw, aa, gg, rwkv_k_k[0], rwkv_k_a[0], rwkv_r_k[0], rwkv_ln_w[0], rwkv_ln_b[0], s0,
                     n_seq=n_seq, nb=nb, nc=t_len // L, L=L, row0=row0)

    y_p, S_p = run_rwkv(LRP, B, 1, T, 0, jnp.zeros((1, B, RH, RN, RN), F32))
    y_s, S_s = run_rwkv(LRS, BS, nbs, TS, MP // (nbs * LRS), state_rwkv_S)
    h, hn = _res((y_p, y_s), rwkv_w_o[0].astype(BF16), h, norm_ffn[1], tm=512, tk=D, hn_dtype=BF16,
                 name="rwkv_out")
    y_prompt, y_sample = _ffn_ple(h, hn, p_pairs[1], 1, *ffn_w, norm_ple[1], *ple_w, norm_final, split_out=MP)

    return (y_prompt.reshape(B, T, D), y_sample.reshape(BS, TS, D),
            C_p, n_p[None], m_p.reshape(1, B, H), S_p, shift_p[None],
            C_s, n_s[None], m_s.reshape(1, BS, H), S_s, shift_s[None])
```

```python
import functools
import math

import jax
import jax.numpy as jnp
from jax import lax
from jax.experimental import pallas as pl
from jax.experimental.pallas import tpu as pltpu

F32 = jnp.float32
BF16 = jnp.bfloat16

D_MODEL = 2048
MLSTM_HEADS = 8
MLSTM_DQK = 128
MLSTM_DV = 256
MLSTM_CHUNK = 256
GATE_SOFTCAP = 15.0
RWKV_HEAD_DIM = 64
RWKV_HEADS = 32
RWKV_PAIRS = RWKV_HEADS // 2
RWKV_CHUNK = 64
NORM_EPS = 1e-6
RWKV_GN_EPS = 64e-5
L2_EPS = 1e-12
LANES = 128
SUBLANES = 8
VMEM_LIMIT_BYTES = 56 * 1024 * 1024


def _params(*sem):
    return pltpu.CompilerParams(dimension_semantics=sem, vmem_limit_bytes=VMEM_LIMIT_BYTES)


def _row_tile(m, preferred):
    t = preferred
    while m % t:
        t -= LANES
    return t


def _rms(x, g):
    return x * lax.rsqrt(jnp.mean(x * x, axis=-1, keepdims=True) + NORM_EPS) * g


def _dot(a, b):
    return jnp.dot(a, b, preferred_element_type=F32)


def _dot_nt(a, b):
    return lax.dot_general(a, b, (((1,), (1,)), ((), ())), preferred_element_type=F32)


def _dot_tn(a, b):
    return lax.dot_general(a, b, (((0,), (0,)), ((), ())), preferred_element_type=F32)


def _split3(x):
    hi = x.astype(BF16)
    r1 = x - hi.astype(F32)
    mid = r1.astype(BF16)
    lo = (r1 - mid.astype(F32)).astype(BF16)
    return hi, mid, lo


def _dot_sel(x, sel):
    hi, mid, lo = _split3(x)
    return _dot(hi, sel) + _dot(mid, sel) + _dot(lo, sel)


def _sel_dot(sel, x):
    hi, mid, lo = _split3(x)
    return _dot(sel, hi) + _dot(sel, mid) + _dot(sel, lo)


def _sel_dot_nt(x, sel):
    hi, mid, lo = _split3(x)
    return _dot_nt(hi, sel) + _dot_nt(mid, sel) + _dot_nt(lo, sel)


def _row_specs(x, tm, cols, col_index):
    if isinstance(x, tuple):
        a, b = x
        na = a.shape[0] // tm
        return ([pl.BlockSpec((tm, cols), lambda i, *r: (jnp.minimum(i, na - 1), col_index(*r))),
                 pl.BlockSpec((tm, cols), lambda i, *r: (jnp.maximum(i - na, 0), col_index(*r)))], [a, b], na)
    return [pl.BlockSpec((tm, cols), lambda i, *r: (i, col_index(*r)))], [x], None


def _rows(x):
    return sum(a.shape[0] for a in x) if isinstance(x, tuple) else x.shape[0]


def _pair_row_tile(operands, preferred, extra_rows=()):
    halves = [a.shape[0] for x in operands if x is not None for a in (x if isinstance(x, tuple) else (x,))]
    return _row_tile(math.gcd(*halves, *extra_rows), preferred)


def _load_rows(refs, na, dtype=None):
    vals = [r[...] if dtype is None else r[...].astype(dtype) for r in refs]
    if na is None:
        return vals[0]
    return jnp.where(pl.program_id(0) < na, vals[0], vals[1])


def _store_rows(refs, na, val):
    if na is None:
        refs[0][...] = val.astype(refs[0].dtype)
        return

    @pl.when(pl.program_id(0) < na)
    def _():
        refs[0][...] = val.astype(refs[0].dtype)

    @pl.when(pl.program_id(0) >= na)
    def _():
        refs[1][...] = val.astype(refs[1].dtype)


_zero_col = lambda *r: 0


def _norm_kernel(*refs, na):
    nx = 1 if na is None else 2
    g_ref, o_ref = refs[nx], refs[nx + 1]
    o_ref[...] = _rms(_load_rows(refs[:nx], na), g_ref[...]).astype(o_ref.dtype)


def _norm(x, g, out_dtype, tm=512):
    m = _rows(x)
    d = g.shape[0]
    tm = _pair_row_tile([x], tm)
    x_specs, x_args, na = _row_specs(x, tm, d, _zero_col)
    return pl.pallas_call(
        functools.partial(_norm_kernel, na=na),
        grid=(m // tm,),
        in_specs=x_specs + [pl.BlockSpec((1, d), lambda i: (0, 0))],
        out_specs=pl.BlockSpec((tm, d), lambda i: (i, 0)),
        out_shape=jax.ShapeDtypeStruct((m, d), out_dtype),
        compiler_params=_params("parallel"),
        name="rmsnorm",
    )(*x_args, g.reshape(1, d))


def _proj_kernel(*refs, nw, ne, epilogue):
    x_ref = refs[0]
    ws = refs[1:1 + nw]
    es = refs[1 + nw:1 + nw + ne]
    o_ref = refs[1 + nw + ne]
    xb = x_ref[...]
    accs = [_dot(xb, w[...].astype(BF16)) for w in ws]
    o_ref[...] = epilogue(*accs, *[e[...] for e in es]).astype(o_ref.dtype)


def _proj(x, ws, *, out_dtype, tm, tn, extras=(), epilogue=lambda a: a, layer=None, name="proj"):
    m, k = x.shape
    n = ws[0].shape[-1]
    tm = _row_tile(m, tm)
    if layer is None:
        w_spec = pl.BlockSpec((k, tn), lambda i, j: (0, j))
    else:
        w_spec = pl.BlockSpec((None, k, tn), lambda i, j: (layer, 0, j))
    in_specs = ([pl.BlockSpec((tm, k), lambda i, j: (i, 0))]
                + [w_spec for _ in ws]
                + [pl.BlockSpec((1, tn), lambda i, j: (0, j)) for _ in extras])
    return pl.pallas_call(
        functools.partial(_proj_kernel, nw=len(ws), ne=len(extras), epilogue=epilogue),
        grid=(m // tm, n // tn),
        in_specs=in_specs,
        out_specs=pl.BlockSpec((tm, tn), lambda i, j: (i, j)),
        out_shape=jax.ShapeDtypeStruct((m, n), out_dtype),
        compiler_params=_params("parallel", "arbitrary"),
        name=name,
    )(x, *ws, *[e.reshape(1, n) for e in extras])


def _res_kernel(*refs, nk, gated, emit_h, na_x, na_res, na_x2, na_out):
    pos = [0]

    def take(n):
        out = refs[pos[0]:pos[0] + n]
        pos[0] += n
        return out

    width = lambda na: 1 if na is None else 2
    x_refs = take(width(na_x))
    (w_ref,) = take(1)
    res_refs = take(width(na_res))
    (g_ref,) = take(1)
    if gated:
        x2_refs = take(width(na_x2))
        (w2_ref,) = take(1)
    h_refs = take(1) if emit_h else ()
    hn_refs = take(width(na_out))
    acc_refs = take(1) if nk > 1 else ()
    k = pl.program_id(1)

    part = _dot(_load_rows(x_refs, na_x, BF16), w_ref[...])
    if nk > 1:
        acc_ref = acc_refs[0]

        @pl.when(k == 0)
        def _():
            acc_ref[...] = part

        @pl.when(k > 0)
        def _():
            acc_ref[...] += part

    @pl.when(k == nk - 1)
    def _():
        a = acc_refs[0][...] if nk > 1 else part
        res = _load_rows(res_refs, na_res)
        if gated:
            h = res + jax.nn.sigmoid(a) * _dot(_load_rows(x2_refs, na_x2, BF16), w2_ref[...])
        else:
            h = res + a
        if emit_h:
            h_refs[0][...] = h
        _store_rows(hn_refs, na_out, _rms(h, g_ref[...]))


def _res(x, w, res, gain, *, tm, tk, hn_dtype, x2=None, w2=None, emit_h=True, split_out=None, layer=None,
         name="res"):
    m = _rows(x)
    kdim, d = w.shape[-2:]
    lead = () if layer is None else (None,)
    at_layer = (lambda *idx: idx) if layer is None else (lambda *idx: (layer,) + idx)
    tm = _pair_row_tile([x, res, x2], tm, () if split_out is None else (split_out, m - split_out))
    nk = kdim // tk
    gated = x2 is not None
    x_specs, x_args, na_x = _row_specs(x, tm, tk, lambda k: k)
    res_specs, res_args, na_res = _row_specs(res, tm, d, _zero_col)
    const = lambda i, k: (0, 0)
    w_mode = dict(pipeline_mode=pl.Buffered(1)) if nk == 1 else {}
    in_specs = (x_specs + [pl.BlockSpec(lead + (tk, d), lambda i, k: at_layer(k, 0), **w_mode)]
                + res_specs + [pl.BlockSpec((1, d), const)])
    args = x_args + [w] + res_args + [gain.reshape(1, d)]
    na_x2 = None
    if gated:
        k2 = w2.shape[-2]
        x2_specs, x2_args, na_x2 = _row_specs(x2, tm, k2, _zero_col)
        in_specs += x2_specs + [pl.BlockSpec(lead + (k2, d), lambda i, k: at_layer(0, 0),
                                             pipeline_mode=pl.Buffered(1))]
        args += x2_args + [w2]
    out_specs, out_shape = [], []
    if emit_h:
        out_specs.append(pl.BlockSpec((tm, d), lambda i, k: (i, 0)))
        out_shape.append(jax.ShapeDtypeStruct((m, d), F32))
    na_out = None
    if split_out is None:
        out_specs.append(pl.BlockSpec((tm, d), lambda i, k: (i, 0)))
        out_shape.append(jax.ShapeDtypeStruct((m, d), hn_dtype))
    else:
        na_out = split_out // tm
        out_specs += [pl.BlockSpec((tm, d), lambda i, k: (jnp.minimum(i, na_out - 1), 0)),
                      pl.BlockSpec((tm, d), lambda i, k: (jnp.maximum(i - na_out, 0), 0))]
        out_shape += [jax.ShapeDtypeStruct((split_out, d), hn_dtype),
                      jax.ShapeDtypeStruct((m - split_out, d), hn_dtype)]
    return pl.pallas_call(
        functools.partial(_res_kernel, nk=nk, gated=gated, emit_h=emit_h, na_x=na_x, na_res=na_res,
                          na_x2=na_x2, na_out=na_out),
        grid=(m // tm, nk),
        in_specs=in_specs,
        out_specs=out_specs,
        out_shape=out_shape,
        scratch_shapes=[pltpu.VMEM((tm, d), F32)] if nk > 1 else [],
        compiler_params=_params("parallel", "arbitrary"),
        name=name,
    )(*args)


def _mlstm_kernel(q_ref, k_ref, v_ref, og_ref, g_ref, gt_ref, nw_ref, c0_ref, n0_ref, m0_ref,
                  h_ref, c_ref, n_ref, m_ref, *, L):
    H, DK, DV = MLSTM_HEADS, MLSTM_DQK, MLSTM_DV

    @pl.when(pl.program_id(1) == 0)
    def _():
        c_ref[...] = c0_ref[...]
        n_ref[...] = n0_ref[...]
        m_ref[...] = m0_ref[...]

    q = q_ref[...].astype(BF16)
    k = k_ref[...].astype(BF16)
    v = v_ref[...].astype(BF16)
    og = og_ref[...]
    G = g_ref[...]
    GT = gt_ref[...]
    nw = nw_ref[...]
    m_prev = m_ref[0]

    row = lax.broadcasted_iota(jnp.int32, (L, L), 0)
    col = lax.broadcasted_iota(jnp.int32, (L, L), 1)
    causal = col <= row
    tri = causal.astype(BF16)
    b_col = _sel_dot(tri, G)
    b_row = _sel_dot_nt(GT, tri)
    lane_h = lax.broadcasted_iota(jnp.int32, (1, H), 1)

    hs = range(H)
    qh = [q[:, h * DK:(h + 1) * DK] for h in hs]
    kh = [k[:, h * DK:(h + 1) * DK] for h in hs]
    vh = [v[:, h * DV:(h + 1) * DV] for h in hs]
    ch = [c_ref[0, h] for h in hs]
    nh = [n_ref[0, h:h + 1, :] for h in hs]
    bc = [b_col[:, H + h:H + h + 1] for h in hs]
    li_c = [G[:, h:h + 1] for h in hs]
    m_h = [m_prev[:, h:h + 1] for h in hs]
    qk = [_dot_nt(qh[h], kh[h]) for h in hs]
    qc = [_dot(qh[h], ch[h].astype(BF16)) for h in hs]
    dlog = [jnp.where(causal, bc[h] - b_row[H + h:H + h + 1, :] + GT[h:h + 1, :], -jnp.inf) for h in hs]
    a = [bc[h] + m_h[h] for h in hs]
    m_t = [jnp.maximum(a[h], jnp.max(dlog[h], axis=1, keepdims=True)) for h in hs]
    w_inter = [jnp.exp(a[h] - m_t[h]) for h in hs]
    s = [qk[h] * jnp.exp(dlog[h] - m_t[h]) for h in hs]
    sv = [_dot(s[h].astype(BF16), vh[h]) for h in hs]
    m_new = [m_t[h][L - 1:L, :] for h in hs]
    b_last = [bc[h][L - 1:L, :] for h in hs]
    wk = [jnp.exp(b_last[h] - bc[h] + li_c[h] - m_new[h]) for h in hs]
    kv = [_dot_tn(kh[h], (vh[h].astype(F32) * wk[h]).astype(BF16)) for h in hs]
    decay = [jnp.exp(b_last[h] + m_h[h] - m_new[h]) for h in hs]
    for h in hs:
        c_ref[0, h] = decay[h] * ch[h] + kv[h]
        n_ref[0, h:h + 1, :] = decay[h] * nh[h] + jnp.sum(kh[h].astype(F32) * wk[h], axis=0, keepdims=True)
    m_out = jnp.zeros((1, H), F32)
    for h in hs:
        m_out = jnp.where(lane_h == h, m_new[h], m_out)
    m_ref[0] = m_out

    for h in hs:
        num = w_inter[h] * qc[h] + sv[h]
        den = (w_inter[h] * jnp.sum(qh[h].astype(F32) * nh[h], axis=1, keepdims=True)
               + jnp.sum(s[h], axis=1, keepdims=True))
        hh = num / jnp.maximum(jnp.abs(den), jnp.exp(-m_t[h]))
        hh = hh * lax.rsqrt(jnp.mean(hh * hh, axis=1, keepdims=True) + NORM_EPS)
        o = jax.nn.sigmoid(og[:, h * DV:(h + 1) * DV].astype(F32))
        h_ref[:, h * DV:(h + 1) * DV] =(o * (hh * nw[:, h * DV:(h + 1) * DV])).astype(h_ref.dtype)


def _mlstm(qkvo, g3, gt3, nw, c0, n0, m0, *, n_seq, nc, L, row0, out_dtype):
    H, DK, DV = MLSTM_HEADS, MLSTM_DQK, MLSTM_DV
    nqk, nv = H * DK, H * DV
    flat = qkvo.ndim == 2

    def rows(colblk):
        return lambda s, c: (row0 + s * nc + c, 0, colblk)

    def tok(width, colblk):
        if flat:
            return pl.BlockSpec((L, width), lambda s, c: (row0 + s * nc + c, colblk))
        return pl.BlockSpec((None, L, width), rows(colblk))

    state = lambda s, c: (0, s, 0, 0, 0)
    state3 = lambda s, c: (s, 0, 0)
    in_specs = [
        tok(nqk, 0),
        tok(nqk, 1),
        tok(nv, 1),
        tok(nv, 2),
        pl.BlockSpec((None, L, LANES), rows(0)),
        pl.BlockSpec((None, LANES, L), rows(0)),
        pl.BlockSpec((1, nv), lambda s, c: (0, 0)),
        pl.BlockSpec((None, 1, H, DK, DV), state),
        pl.BlockSpec((1, H, DK), state3),
        pl.BlockSpec((1, 1, H), state3),
    ]
    out_specs = [
        (pl.BlockSpec((L, nv), lambda s, c: (s * nc + c, 0)) if flat
         else pl.BlockSpec((None, L, nv), lambda s, c: (s * nc + c, 0, 0))),
        pl.BlockSpec((None, 1, H, DK, DV), state),
        pl.BlockSpec((1, H, DK), state3),
        pl.BlockSpec((1, 1, H), state3),
    ]
    out_shape = [
        jax.ShapeDtypeStruct((n_seq * nc * L, nv) if flat else (n_seq * nc, L, nv), out_dtype),
        jax.ShapeDtypeStruct((1, n_seq, H, DK, DV), F32),
        jax.ShapeDtypeStruct((n_seq, H, DK), F32),
        jax.ShapeDtypeStruct((n_seq, 1, H), F32),
    ]
    return pl.pallas_call(
        functools.partial(_mlstm_kernel, L=L),
        grid=(n_seq, nc),
        in_specs=in_specs,
        out_specs=out_specs,
        out_shape=out_shape,
        compiler_params=_params("parallel", "arbitrary"),
        name="mlstm_chunks",
    )(qkvo, qkvo, qkvo, qkvo, g3, gt3, nw.reshape(1, nv), c0, n0, m0.reshape(n_seq, 1, H))


def _shifted(x_ref, prev_ref, sx_ref, *, tm, t_prompt, t_sample, n_prompt_tiles):
    i = pl.program_id(0)
    x = x_ref[...]
    rolled = pltpu.roll(x, 1, axis=0)
    rowi = lax.broadcasted_iota(jnp.int32, (tm, 1), 0)
    first = jnp.where((i * tm) % t_prompt == 0, 0.0, prev_ref[SUBLANES - 1:SUBLANES, :])
    xprev = jnp.where(rowi == 0, first, rolled)
    at_sample_start = jnp.logical_and(i >= n_prompt_tiles, rowi % t_sample == 0)
    return x, jnp.where(at_sample_start, sx_ref[...], xprev)


def _shift_specs(tm, d, n_prompt_tiles):
    return [pl.BlockSpec((tm, d), lambda i, *r: (i, 0)),
            pl.BlockSpec((SUBLANES, d), lambda i, *r: (jnp.maximum(i * (tm // SUBLANES) - 1, 0), 0)),
            pl.BlockSpec((tm, d), lambda i, *r: (jnp.maximum(i - n_prompt_tiles, 0), 0))]


def _rkv_kernel(x_ref, prev_ref, sx_ref, mu_ref, w_ref, o_ref, xb_ref, *, tiles_per_group, shift):
    j = pl.program_id(1)

    @pl.when(j == 0)
    def _():
        x, xp = _shifted(x_ref, prev_ref, sx_ref, **shift)
        dx = xp - x
        for c in range(3):
            xb_ref[c] = (x + dx * mu_ref[c:c + 1, :]).astype(BF16)

    o_ref[...] = _dot(xb_ref[j // tiles_per_group], w_ref[...]).astype(o_ref.dtype)


def _rkv(xn, sx, mu3, w_rkv, *, tm, tn, shift):
    m, d = xn.shape
    n = w_rkv.shape[1]
    return pl.pallas_call(
        functools.partial(_rkv_kernel, tiles_per_group=d // tn, shift=shift),
        grid=(m // tm, n // tn),
        in_specs=_shift_specs(tm, d, shift["n_prompt_tiles"]) + [
            pl.BlockSpec((3, d), lambda i, j: (0, 0)), pl.BlockSpec((d, tn), lambda i, j: (0, j))],
        out_specs=pl.BlockSpec((tm, tn), lambda i, j: (i, j)),
        out_shape=jax.ShapeDtypeStruct((m, n), BF16),
        scratch_shapes=[pltpu.VMEM((3, tm, d), BF16)],
        compiler_params=_params("parallel", "arbitrary"),
        name="rwkv_rkv",
    )(xn, xn, sx, mu3, w_rkv)


def _lora_kernel(x_ref, prev_ref, sx_ref, mu_ref, w1_ref, w2_ref, b_ref, lw_ref, a_ref, g_ref, last_ref, *,
                 shift):
    x, xp = _shifted(x_ref, prev_ref, sx_ref, **shift)
    dx = xp - x
    n_last, tm = last_ref.shape[0], x.shape[0]
    pick = (lax.broadcasted_iota(jnp.int32, (n_last, tm), 1)
            == lax.broadcasted_iota(jnp.int32, (n_last, tm), 0) * (tm // n_last) + (tm // n_last - 1))
    last_ref[...] = _sel_dot(pick.astype(BF16), x)

    def branch(c, mid):
        xm = (x + dx * mu_ref[c:c + 1, :]).astype(BF16)
        hid = mid(_dot(xm, w1_ref[c])).astype(BF16)
        return b_ref[c:c + 1, :] + _dot(hid, w2_ref[c])

    lw_ref[...] = -math.exp(-0.5) * jax.nn.sigmoid(branch(0, jnp.tanh))
    a_ref[...] = jax.nn.sigmoid(branch(1, lambda z: z))
    g_ref[...] = branch(2, jax.nn.sigmoid).astype(g_ref.dtype)


def _lora(xn, sx, mu3, w1s, w2s, bias3, *, tm, shift):
    m, d = xn.shape
    r = w1s.shape[2]
    row = pl.BlockSpec((tm, d), lambda i: (i, 0))
    return pl.pallas_call(
        functools.partial(_lora_kernel, shift=shift),
        grid=(m // tm,),
        in_specs=_shift_specs(tm, d, shift["n_prompt_tiles"]) + [
            pl.BlockSpec((3, d), lambda i: (0, 0)), pl.BlockSpec((3, d, r), lambda i: (0, 0, 0)),
            pl.BlockSpec((3, r, d), lambda i: (0, 0, 0)), pl.BlockSpec((3, d), lambda i: (0, 0))],
        out_specs=[row, row, row, pl.BlockSpec((tm // shift["t_sample"], d), lambda i: (i, 0))],
        out_shape=[jax.ShapeDtypeStruct((m, d), F32), jax.ShapeDtypeStruct((m, d), F32),
                   jax.ShapeDtypeStruct((m, d), BF16), jax.ShapeDtypeStruct((m // shift["t_sample"], d), F32)],
        compiler_params=_params("parallel"),
        name="rwkv_lora",
    )(xn, xn, sx, mu3, w1s, w2s, bias3)


def _rwkv_kernel(r_ref, k_ref, v_ref, lw_ref, a_ref, g_ref, kk_ref, ka_ref, rk_ref, lnw_ref, lnb_ref,
                 e_ref, et_ref, s0_ref, o_ref, s_ref, sbd_ref, y_ref, *, L, nb, nc):
    N = RWKV_HEAD_DIM
    R = nb * L
    R2 = 2 * R
    c = pl.program_id(1)
    lane = lax.broadcasted_iota(jnp.int32, (1, LANES), 1)
    head0 = lane < N
    seqs = range(nb)
    pairs = range(RWKV_PAIRS)

    @pl.when(c == 0)
    def _():
        z = jnp.zeros((N, N), F32)
        for j in seqs:
            for p in pairs:
                top = jnp.concatenate([s0_ref[j, 2 * p], z], axis=1)
                bot = jnp.concatenate([z, s0_ref[j, 2 * p + 1]], axis=1)
                sbd_ref[j, p] = jnp.concatenate([top, bot], axis=0)

    E = e_ref[...]
    ET = et_ref[...]

    def seg_sum_bcast(x):
        return _dot_sel(_dot_sel(x, E), ET)

    r = r_ref[...].astype(F32)
    k = k_ref[...].astype(F32)
    v = v_ref[...].astype(F32)
    lw = lw_ref[...]
    a = a_ref[...]
    kk = k * kk_ref[...]
    kk = kk / jnp.maximum(jnp.sqrt(seg_sum_bcast(kk * kk)), L2_EPS)
    k2 = k * (1.0 + (a - 1.0) * ka_ref[...])

    row = lax.broadcasted_iota(jnp.int32, (R, R), 0)
    col = lax.broadcasted_iota(jnp.int32, (R, R), 1)
    tri = ((col <= row) & (row // L == col // L)).astype(BF16)
    logp = _sel_dot(tri, lw)
    p_incl = jnp.exp(logp)
    inv_p = jnp.exp(-logp)
    at = -kk * jnp.exp(logp - lw)
    bt = kk * a * inv_p
    kt = k2 * inv_p
    rt = r * p_incl

    row2 = lax.broadcasted_iota(jnp.int32, (R2, R2), 0)
    col2 = lax.broadcasted_iota(jnp.int32, (R2, R2), 1)
    same = row2 // L == col2 // L
    strict = same & (col2 < row2)
    incl = same & (col2 <= row2)
    eye = (row2 == col2).astype(F32)
    n_sq = max(int(math.ceil(math.log2(L))) - 1, 0)

    def stack(x):
        parts = []
        for j in seqs:
            xj = x[j * L:(j + 1) * L]
            parts += [jnp.where(head0, xj, 0.0), jnp.where(head0, 0.0, xj)]
        return jnp.concatenate(parts, axis=0)

    sls = [slice(p * LANES, (p + 1) * LANES) for p in pairs]
    seq_rows = [slice(j * 2 * L, (j + 1) * 2 * L) for j in seqs]
    xa = [stack(at[:, s]).astype(BF16) for s in sls]
    xr = [stack(rt[:, s]).astype(BF16) for s in sls]
    yb = [stack(bt[:, s]).astype(BF16) for s in sls]
    yk = [stack(kt[:, s]).astype(BF16) for s in sls]
    vr = [stack(v[:, s]).astype(BF16) for s in sls]
    sbd = [[sbd_ref[j, p] for p in pairs] for j in seqs]
    sb = [[x.astype(BF16) for x in sj] for sj in sbd]

    big = [_dot_nt(jnp.concatenate([xa[p], xr[p]], axis=0), jnp.concatenate([yb[p], yk[p]], axis=0))
           for p in pairs]
    n_ab = [jnp.where(strict, big[p][:R2, :R2], 0.0) for p in pairs]
    a_akrk = [jnp.concatenate([jnp.where(strict, big[p][:R2, R2:], 0.0), jnp.where(incl, big[p][R2:, R2:], 0.0)],
                              axis=0).astype(BF16) for p in pairs]
    a_rb = [jnp.where(incl, big[p][R2:, :R2], 0.0).astype(BF16) for p in pairs]

    t_inv = [eye + n for n in n_ab]
    pw = [n.astype(BF16) for n in n_ab]
    for _ in range(n_sq):
        pw = [_dot(x, x).astype(BF16) for x in pw]
        t_inv = [t + _dot(t.astype(BF16), x) for t, x in zip(t_inv, pw)]

    xs = [[_dot_nt(jnp.concatenate([xa[p][seq_rows[j]], xr[p][seq_rows[j]]], axis=0), sb[j][p]) for j in seqs]
          for p in pairs]
    xsa = [jnp.concatenate([xs[p][j][:2 * L] for j in seqs], axis=0) for p in pairs]
    xsr = [jnp.concatenate([xs[p][j][2 * L:] for j in seqs], axis=0) for p in pairs]
    av = [_dot(a_akrk[p], vr[p]) for p in pairs]
    ub = [_dot(t_inv[p].astype(BF16), (xsa[p] + av[p][:R2]).astype(BF16)).astype(BF16) for p in pairs]
    yr = [xsr[p] + _dot(a_rb[p], ub[p]) + av[p][R2:] for p in pairs]
    for p in pairs:
        for j in seqs:
            lo = j * 2 * L
            y_ref[j * L:(j + 1) * L, sls[p]] = yr[p][lo:lo + L] + yr[p][lo + L:lo + 2 * L]
    for j in seqs:
        p_last = p_incl[(j + 1) * L - 1:(j + 1) * L, :]
        for p in pairs:
            upd = _dot_tn(jnp.concatenate([ub[p][seq_rows[j]], vr[p][seq_rows[j]]], axis=0),
                          jnp.concatenate([yb[p][seq_rows[j]], yk[p][seq_rows[j]]], axis=0))
            sbd_ref[j, p] = (sbd[j][p] + upd) * p_last[:, sls[p]]

    y = y_ref[...]
    mean = seg_sum_bcast(y) * (1.0 / N)
    yc = y - mean
    var = seg_sum_bcast(yc * yc) * (1.0 / N)
    yn = yc * lax.rsqrt(var + RWKV_GN_EPS) * lnw_ref[...] + lnb_ref[...]
    bonus = seg_sum_bcast(r * k2 * rk_ref[...]) * v
    o_ref[...] = ((yn + bonus) * g_ref[...].astype(F32)).astype(o_ref.dtype)

    @pl.when(c == nc - 1)
    def _():
        for j in seqs:
            for p in pairs:
                blk_p = sbd_ref[j, p]
                s_ref[j, 2 * p] = blk_p[:N, :N]
                s_ref[j, 2 * p + 1] = blk_p[N:, N:]


def _rwkv(rkv, lw, aa, gg, k_k, k_a, r_k, ln_w, ln_b, s0, *, n_seq, nb, nc, L, row0):
    D, H, N = D_MODEL, RWKV_HEADS, RWKV_HEAD_DIM
    R = nb * L
    assert nb == 1 or nc == 1
    onehot = (jnp.arange(D)[:, None] // N == jnp.arange(H)[None, :]).astype(BF16)

    def rows(colblk):
        return lambda s, c: (row0 + s * nc + c, colblk)

    vec = pl.BlockSpec((1, D), lambda s, c: (0, 0))
    state = lambda s, c: (0, s, 0, 0, 0)
    in_specs = [
        pl.BlockSpec((R, D), rows(0)), pl.BlockSpec((R, D), rows(1)), pl.BlockSpec((R, D), rows(2)),
        pl.BlockSpec((R, D), rows(0)), pl.BlockSpec((R, D), rows(0)), pl.BlockSpec((R, D), rows(0)),
        vec, vec, vec, vec, vec,
        pl.BlockSpec((D, H), lambda s, c: (0, 0)), pl.BlockSpec((H, D), lambda s, c: (0, 0)),
        pl.BlockSpec((None, nb, H, N, N), state),
    ]
    out_specs = [
        pl.BlockSpec((R, D), lambda s, c: (s * nc + c, 0)),
        pl.BlockSpec((None, nb, H, N, N), state),
    ]
    out_shape = [
        jax.ShapeDtypeStruct((n_seq * nc * L, D), BF16),
        jax.ShapeDtypeStruct((1, n_seq, H, N, N), F32),
    ]
    return pl.pallas_call(
        functools.partial(_rwkv_kernel, L=L, nb=nb, nc=nc),
        grid=(n_seq // nb, nc),
        in_specs=in_specs,
        out_specs=out_specs,
        out_shape=out_shape,
        scratch_shapes=[pltpu.VMEM((nb, RWKV_PAIRS, LANES, LANES), F32), pltpu.VMEM((R, D), F32)],
        compiler_params=_params("parallel", "arbitrary"),
        name="rwkv7_chunks",
    )(rkv, rkv, rkv, lw, aa, gg, k_k.reshape(1, D), k_a.reshape(1, D), r_k.reshape(1, D),
      ln_w.reshape(1, D), ln_b.reshape(1, D), onehot, onehot.T, s0)


def _mlstm_gate_epilogue(acc, bias):
    z = GATE_SOFTCAP * jnp.tanh((acc + bias) / GATE_SOFTCAP)
    lane = lax.broadcasted_iota(jnp.int32, z.shape, 1)
    return jnp.where(lane < MLSTM_HEADS, z, jax.nn.log_sigmoid(z))


def _ffn_ple(h, hn, p_pair, layer, w_gate, w_up, w_down, g_ple, w_ple_gate, w_ple_proj, next_gain, *, split_out):
    d_ff, d = w_down.shape[-2:]
    act = _proj(hn, [w_gate, w_up], out_dtype=BF16, tm=2304, tn=512, layer=layer,
                epilogue=lambda g, u: jax.nn.silu(g) * u, name="swiglu_up")
    h, hn = _res(act, w_down, h, g_ple, tm=256, tk=d_ff, hn_dtype=BF16, layer=layer, name="swiglu_down")
    return _res(hn, w_ple_gate, h, next_gain, tm=512, tk=d, hn_dtype=F32, x2=p_pair, w2=w_ple_proj,
                emit_h=split_out is None, split_out=split_out, layer=layer, name="ple")


def kernel(x_prompt, x_sample, state_mlstm_C, state_mlstm_n, state_mlstm_m, state_rwkv_S, state_rwkv_shift,
           p_prompt, p_sample, norm_mix, norm_ffn, norm_ple, norm_final, ffn_w_gate, ffn_w_up, ffn_w_down,
           ple_w_proj, ple_w_gate, mlstm_w_q, mlstm_w_k, mlstm_w_v, mlstm_w_igate, mlstm_b_igate,
           mlstm_w_fgate, mlstm_b_fgate, mlstm_w_ogate, mlstm_norm_w, mlstm_w_out, rwkv_mu, rwkv_w_r,
           rwkv_w_k, rwkv_w_v, rwkv_w_o, rwkv_w0, rwkv_w1, rwkv_w2, rwkv_a0, rwkv_a1, rwkv_a2, rwkv_g1,
           rwkv_g2, rwkv_k_k, rwkv_k_a, rwkv_r_k, rwkv_ln_w, rwkv_ln_b):
    D = D_MODEL
    B, T, _ = x_prompt.shape
    BS, TS, _ = x_sample.shape
    MP, MS = B * T, BS * TS
    M = MP + MS
    H, DK, DV = MLSTM_HEADS, MLSTM_DQK, MLSTM_DV
    RH, RN = RWKV_HEADS, RWKV_HEAD_DIM
    PD = p_prompt.shape[-1]

    x_pair = (x_prompt.reshape(MP, D), x_sample.reshape(MS, D))
    p_pairs = [(p_prompt[i].reshape(MP, PD), p_sample[i].reshape(MS, PD)) for i in range(2)]

    hn = _norm(x_pair, norm_mix[0], BF16)
    w_qkvo = jnp.concatenate([mlstm_w_q[0], mlstm_w_k[0], mlstm_w_v[0], mlstm_w_ogate[0]], axis=1).astype(BF16)
    col_scale = jnp.concatenate([jnp.ones((H * DK,), F32), jnp.full((H * DK,), DK ** -0.5, F32),
                                 jnp.ones((2 * H * DV,), F32)])
    qkvo = _proj(hn, [w_qkvo], out_dtype=BF16, tm=2304, tn=1024, extras=[col_scale],
                 epilogue=lambda a, s: a * s, name="mlstm_qkvo")
    w_gates = jnp.zeros((D, LANES), F32).at[:, :H].set(mlstm_w_igate[0]).at[:, H:2 * H].set(mlstm_w_fgate[0])
    b_gates = jnp.zeros((LANES,), F32).at[:H].set(mlstm_b_igate[0]).at[H:2 * H].set(mlstm_b_fgate[0])
    gates = _proj(hn, [w_gates.astype(BF16)], out_dtype=F32, tm=1024, tn=LANES, extras=[b_gates],
                  epilogue=_mlstm_gate_epilogue, name="mlstm_gates")

    LP = math.gcd(T, MLSTM_CHUNK)
    LS = math.gcd(TS, MLSTM_CHUNK)
    gp = gates.reshape(M // LP, LP, LANES)
    gs = gates[MP:].reshape(MS // LS, LS, LANES)
    hm_p, C_p, n_p, m_p = _mlstm(qkvo, gp, gp.transpose(0, 2, 1), mlstm_norm_w[0],
                                 jnp.zeros((1, B, H, DK, DV), F32), jnp.zeros((B, H, DK), F32),
                                 jnp.zeros((B, H), F32), n_seq=B, nc=T // LP, L=LP, row0=0, out_dtype=BF16)
    hm_s, C_s, n_s, m_s = _mlstm(qkvo[MP:].astype(F32).reshape(MS // LS, LS, -1), gs, gs.transpose(0, 2, 1),
                                 mlstm_norm_w[0], state_mlstm_C, state_mlstm_n[0], state_mlstm_m[0],
                                 n_seq=BS, nc=TS // LS, L=LS, row0=0, out_dtype=F32)
    h, hn = _res((hm_p, hm_s.reshape(MS, D)), mlstm_w_out[0].astype(BF16), x_pair, norm_ffn[0],
                 tm=512, tk=D, hn_dtype=BF16, name="mlstm_out")
    ffn_w = (ffn_w_gate, ffn_w_up, ffn_w_down.astype(BF16))
    ple_w = (ple_w_gate.astype(BF16), ple_w_proj.astype(BF16))
    h, xn = _ffn_ple(h, hn, p_pairs[0], 0, *ffn_w, norm_ple[0], *ple_w, norm_mix[1], split_out=None)

    tm_r = _row_tile(math.gcd(MP, MS), 512)
    shift = dict(tm=tm_r, t_prompt=T, t_sample=TS, n_prompt_tiles=MP // tm_r)
    sx = jnp.repeat(state_rwkv_shift[0], TS, axis=0)
    mu = rwkv_mu[0]
    w_rkv = jnp.concatenate([rwkv_w_r[0], rwkv_w_k[0], rwkv_w_v[0]], axis=1).astype(BF16)
    rkv = _rkv(xn, sx, jnp.stack([mu[0], mu[2], mu[3]]), w_rkv, tm=tm_r, tn=D, shift=shift)

    rank =max(-(-w.shape[1] // LANES) * LANES for w in (rwkv_w1[0], rwkv_a1[0], rwkv_g1[0]))

    def pad_to(w1, w2):
        r = w1.shape[1]
        return (jnp.pad(w1, ((0, 0), (0, rank - r))).astype(BF16), jnp.pad(w2, ((0, rank - r), (0, 0))).astype(BF16))

    lora_w = [pad_to(rwkv_w1[0], rwkv_w2[0]), pad_to(rwkv_a1[0], rwkv_a2[0]), pad_to(rwkv_g1[0], rwkv_g2[0])]
    lw, aa, gg, last_rows = _lora(xn, sx, jnp.stack([mu[1], mu[4], mu[5]]), jnp.stack([w[0] for w in lora_w]),
                                  jnp.stack([w[1] for w in lora_w]),
                                  jnp.stack([rwkv_w0[0], rwkv_a0[0], jnp.zeros((D,), F32)]), tm=tm_r, shift=shift)
    shift_p = last_rows[T // TS - 1:MP // TS:T // TS]
    shift_s = last_rows[MP // TS:]

    LRP = math.gcd(T, RWKV_CHUNK)
    LRS = math.gcd(TS, RWKV_CHUNK)
    nbs = RWKV_CHUNK // LRS

    def run_rwkv(L, n_seq, nb, t_len, row0, s0):
        return _rwkv(rkv, lw, aa, gg, rwkv_k_k[0], rwkv_k_a[0], rwkv_r_k[0], rwkv_ln_w[0], rwkv_ln_b[0], s0,
                     n_seq=n_seq, nb=nb, nc=t_len // L, L=L, row0=row0)

    y_p, S_p = run_rwkv(LRP, B, 1, T, 0, jnp.zeros((1, B, RH, RN, RN), F32))
    y_s, S_s = run_rwkv(LRS, BS, nbs, TS, MP // (nbs * LRS), state_rwkv_S)
    h, hn = _res((y_p, y_s), rwkv_w_o[0].astype(BF16), h, norm_ffn[1], tm=512, tk=D, hn_dtype=BF16,
                 name="rwkv_out")
    y_prompt, y_sample = _ffn_ple(h, hn, p_pairs[1], 1, *ffn_w, norm_ple[1], *ple_w, norm_final, split_out=MP)

    return (y_prompt.reshape(B, T, D), y_sample.reshape(BS, TS, D),
            C_p, n_p[None], m_p.reshape(1, B, H), S_p, shift_p[None],
            C_s, n_s[None], m_s.reshape(1, BS, H), S_s, shift_s[None])
```
